```python
import jax, jax.numpy as jnp
from jax import lax
import numpy as np

D_MODEL = 1024
BATCH = 4
SEQ = 4096
DEPTH = 4
DEC_BATCH = 32
DEC_SEQ = 1
PAST_LEN = 8192
PAGE_SIZE = 128

HEAD_DIM = 64
MIX_WIDTH = D_MODEL
N_HEADS = MIX_WIDTH // HEAD_DIM
H_A = N_HEADS // 2
H_B = N_HEADS - H_A
W_A = H_A * HEAD_DIM
W_B = H_B * HEAD_DIM
DILATED_PATTERNS = ((128, 1), (512, 4), (2048, 16))
WIN_MAX = max(w for w, _ in DILATED_PATTERNS)
BAND_BLOCK = 128
ROPE_THETA = 500000.0
ROT_DIM = HEAD_DIM // 4
MLSTM_CHUNK = 128
D_FF = 11 * D_MODEL // 4
CONV_W = 3
PLE_DIM = 256
RMS_EPS = 1e-6
NEG_INF = -1e30
IN_SPLITS = (W_A, 2 * W_A, 3 * W_A, 3 * W_A + W_B, 3 * W_A + 2 * W_B, 3 * W_A + 3 * W_B, 3 * W_A + 4 * W_B)
IN_COLS = 3 * W_A + 4 * W_B + 2 * H_B

kernel_name = 'dilated_mlstm_hybrid_step'


def rmsnorm(x, g):
    xf = x.astype(jnp.float32)
    y = xf * lax.rsqrt(jnp.mean(xf * xf, axis=-1, keepdims=True) + RMS_EPS)
    return (y * g.astype(jnp.float32)).astype(x.dtype)


def partial_rope(x, pos):
    half = ROT_DIM // 2
    inv = ROPE_THETA ** (-jnp.arange(half, dtype=jnp.float32) / half)
    ang = pos.astype(jnp.float32)[:, None] * inv[None, :]
    cos = jnp.cos(ang)[None, :, None, :]
    sin = jnp.sin(ang)[None, :, None, :]
    x1 = x[..., :half].astype(jnp.float32)
    x2 = x[..., half:ROT_DIM].astype(jnp.float32)
    r1 = (x1 * cos - x2 * sin).astype(x.dtype)
    r2 = (x2 * cos + x1 * sin).astype(x.dtype)
    return jnp.concatenate([r1, r2, x[..., ROT_DIM:]], axis=-1)


def mixer_inputs(h, w_in, q_norm, k_norm, b_gates, pos):
    B, T, _ = h.shape
    z = h @ w_in
    qa, ka, va, qb, kb, vb, ob, gates = jnp.split(z, IN_SPLITS, axis=-1)
    qa = partial_rope(rmsnorm(qa.reshape(B, T, H_A, HEAD_DIM), q_norm), pos)
    ka = partial_rope(rmsnorm(ka.reshape(B, T, H_A, HEAD_DIM), k_norm), pos)
    va = va.reshape(B, T, H_A, HEAD_DIM)
    to_heads = lambda t: jnp.transpose(t.reshape(B, T, H_B, HEAD_DIM), (0, 2, 1, 3)).astype(jnp.float32)
    qb = to_heads(qb)
    kb = to_heads(kb) * (HEAD_DIM ** -0.5)
    vb = to_heads(vb)
    g = gates.astype(jnp.float32) + b_gates.astype(jnp.float32)
    ig = jnp.transpose(g[..., :H_B], (0, 2, 1))
    lf = jnp.transpose(jax.nn.log_sigmoid(g[..., H_B:]), (0, 2, 1))
    og = jax.nn.sigmoid(ob)
    return qa, ka, va, qb, kb, vb, ig, lf, og


def dilated_band_stats(q, k, v, window, dilation):
    B, S, H, Dh = q.shape
    span = BAND_BLOCK * dilation
    S_pad = -(-S // span) * span
    L = S_pad // dilation
    nb = L // BAND_BLOCK
    pad = ((0, 0), (0, S_pad - S), (0, 0), (0, 0))

    def to_blocks(t):
        t = jnp.pad(t, pad).reshape(B, L, dilation, H, Dh).transpose(0, 2, 1, 3, 4)
        return t.reshape(B, dilation, nb, BAND_BLOCK, H, Dh).astype(jnp.float32)

    def with_prev(t):
        prev = jnp.pad(t, ((0, 0), (0, 0), (1, 0), (0, 0), (0, 0), (0, 0)))[:, :, :-1]
        return jnp.concatenate([prev, t], axis=3)

    qb = to_blocks(q)
    kk = with_prev(to_blocks(k))
    vv = with_prev(to_blocks(v))
    s = jnp.einsum('brnqhd,brnkhd->brnhqk', qb, kk) * (Dh ** -0.5)
    qi = jnp.arange(BAND_BLOCK)[:, None] + BAND_BLOCK
    kj = jnp.arange(2 * BAND_BLOCK)[None, :]
    dist = qi - kj
    band = (dist >= 0) & (dist <= window // dilation)
    first = (jnp.arange(nb)[:, None, None] > 0) | (kj[None] >= BAND_BLOCK)
    mask = band[None] & first
    s = jnp.where(mask[None, None, :, None], s, NEG_INF)
    m = jnp.max(s, axis=-1)
    e = jnp.exp(s - m[..., None])
    l = jnp.sum(e, axis=-1)
    o = jnp.einsum('brnhqk,brnkhd->brnqhd', e, vv) / jnp.transpose(l, (0, 1, 2, 4, 3))[..., None]
    o = o.reshape(B, dilation, L, H, Dh).transpose(0, 2, 1, 3, 4).reshape(B, S_pad, H, Dh)[:, :S]

    def back(t):
        t = jnp.transpose(t, (0, 1, 2, 4, 3)).reshape(B, dilation, L, H)
        return jnp.transpose(t, (0, 2, 1, 3)).reshape(B, S_pad, H)[:, :S]

    return back(m), back(l), o


def dilated_gather_stats(q, k_all, v_all, window, dilation, buf_len):
    T = q.shape[1]
    J = window // dilation + 1
    idx = buf_len + jnp.arange(T)[:, None] - dilation * jnp.arange(J)[None, :]
    valid = idx >= 0
    idxc = jnp.maximum(idx, 0)
    kg = jnp.take(k_all, idxc, axis=1).astype(jnp.float32)
    vg = jnp.take(v_all, idxc, axis=1).astype(jnp.float32)
    s = jnp.einsum('bthd,btjhd->bthj', q.astype(jnp.float32), kg) * (HEAD_DIM ** -0.5)
    s = jnp.where(valid[None, :, None, :], s, NEG_INF)
    m = jnp.max(s, axis=-1)
    e = jnp.exp(s - m[..., None])
    l = jnp.sum(e, axis=-1)
    o = jnp.einsum('bthj,btjhd->bthd', e, vg) / l[..., None]
    return m, l, o


def combine_by_denominators(stats):
    ms = jnp.stack([s[0] for s in stats])
    ls = jnp.stack([s[1] for s in stats])
    os_ = jnp.stack([s[2] for s in stats])
    M = jnp.max(ms, axis=0)
    w = ls * jnp.exp(ms - M)
    return jnp.sum(w[..., None] * os_, axis=0) / jnp.sum(w, axis=0)[..., None]


def mlstm_chunk(carry, inp):
    C, n, m = carry
    q, k, v, ig, lf = inp
    L = q.shape[2]
    b = jnp.cumsum(lf, axis=-1)
    causal = jnp.tril(jnp.ones((L, L), dtype=bool))
    dlog = jnp.where(causal, b[..., :, None] - b[..., None, :] + ig[..., None, :], NEG_INF)
    inter = b + m[..., None]
    m_t = jnp.maximum(inter, jnp.max(dlog, axis=-1))
    dw = jnp.exp(dlog - m_t[..., None])
    iw = jnp.exp(inter - m_t)
    sc = jnp.einsum('bhtk,bhsk->bhts', q, k) * dw
    num = iw[..., None] * jnp.einsum('bhvk,bhtk->bhtv', C, q) + jnp.einsum('bhts,bhsv->bhtv', sc, v)
    den = iw * jnp.einsum('bhk,bhtk->bht', n, q) + jnp.sum(sc, axis=-1)
    h = num / jnp.maximum(jnp.abs(den), jnp.exp(-m_t))[..., None]
    m_new = m_t[..., -1]
    ws = jnp.exp(b[..., -1:] - b + ig - m_new[..., None])
    cw = jnp.exp(b[..., -1] + m - m_new)
    C_new = cw[..., None, None] * C + jnp.einsum('bhs,bhsv,bhsk->bhvk', ws, v, k)
    n_new = cw[..., None] * n + jnp.einsum('bhs,bhsk->bhk', ws, k)
    return (C_new, n_new, m_new), h


def mlstm_prompt(q, k, v, ig, lf):
    B, H, S, Dh = q.shape
    nc = S // MLSTM_CHUNK
    chunks = lambda t: jnp.moveaxis(t.reshape(B, H, nc, MLSTM_CHUNK, *t.shape[3:]), 2, 0)
    init = (jnp.zeros((B, H, Dh, Dh), jnp.float32), jnp.zeros((B, H, Dh), jnp.float32),
            jnp.zeros((B, H), jnp.float32))
    carry, h = lax.scan(mlstm_chunk, init, (chunks(q), chunks(k), chunks(v), chunks(ig), chunks(lf)))
    h = jnp.moveaxis(h, 0, 2).reshape(B, H, S, Dh)
    return h, carry


def merge_heads(x, o_a, h_b, og, g_a, g_b, w_o):
    B, T, _ = x.shape
    ya = rmsnorm(o_a.reshape(B, T, W_A), g_a)
    yb = og.astype(jnp.float32) * rmsnorm(jnp.transpose(h_b, (0, 2, 1, 3)).reshape(B, T, W_B), g_b)
    y = jnp.concatenate([ya, yb], axis=-1).astype(x.dtype)
    return x + y @ w_o


def conv_glu(x, conv_buf, g_norm, w_up, conv_w, conv_b, w_down):
    h = rmsnorm(x, g_norm)
    gate, val = jnp.split(h @ w_up, 2, axis=-1)
    gp = jnp.concatenate([conv_buf.astype(gate.dtype), gate], axis=1)
    T = gate.shape[1]
    gc = conv_b + sum(conv_w[j] * gp[:, j:j + T] for j in range(CONV_W))
    y = jax.nn.gelu(gc) * val
    return x + y @ w_down, gp[:, -(CONV_W - 1):]


def per_layer_embed(x, p, g_norm, w_gate, w_proj):
    gate = jax.nn.sigmoid(rmsnorm(x, g_norm) @ w_gate)
    return x + gate * (p.astype(x.dtype) @ w_proj)


def setup_inputs(seed: int = 0) -> dict:
    key = jax.random.key(seed)
    ks = jax.random.split(key, 28)
    f32 = jnp.float32
    nrm = lambda k, shape, scale: jax.random.normal(k, shape, f32) * scale
    gain = lambda k, shape: 1.0 + 0.05 * jax.random.normal(k, shape, f32)
    w_buf = min(WIN_MAX, PAST_LEN)
    b_gates = jnp.concatenate([nrm(ks[10], (DEPTH, H_B), 0.1),
                               3.0 + 2.0 * jax.random.uniform(ks[11], (DEPTH, H_B), f32)], axis=-1)
    return {
        'x_prompt': nrm(ks[0], (BATCH, SEQ, D_MODEL), 1.0),
        'x_sample': nrm(ks[1], (DEC_BATCH, DEC_SEQ, D_MODEL), 1.0),
        'cache_win_k': nrm(ks[2], (DEPTH, DEC_BATCH, w_buf, H_A, HEAD_DIM), 1.0),
        'cache_win_v': nrm(ks[3], (DEPTH, DEC_BATCH, w_buf, H_A, HEAD_DIM), 1.0),
        'state_C': nrm(ks[4], (DEPTH, DEC_BATCH, H_B, HEAD_DIM, HEAD_DIM), 0.1),
        'state_n': nrm(ks[5], (DEPTH, DEC_BATCH, H_B, HEAD_DIM), 0.5),
        'state_m': nrm(ks[6], (DEPTH, DEC_BATCH, H_B), 0.5),
        'state_conv': nrm(ks[7], (DEPTH, DEC_BATCH, CONV_W - 1, D_FF), 1.0),
        'p_prompt': nrm(ks[8], (DEPTH, BATCH, SEQ, PLE_DIM), 1.0),
        'p_sample': nrm(ks[9], (DEPTH, DEC_BATCH, DEC_SEQ, PLE_DIM), 1.0),
        'norm_mix': gain(ks[12], (DEPTH, D_MODEL)),
        'w_in': nrm(ks[13], (DEPTH, D_MODEL, IN_COLS), D_MODEL ** -0.5),
        'q_norm': gain(ks[14], (DEPTH, HEAD_DIM)),
        'k_norm': gain(ks[15], (DEPTH, HEAD_DIM)),
        'b_gates': b_gates,
        'out_norm_a': gain(ks[16], (DEPTH, W_A)),
        'out_norm_b': gain(ks[17], (DEPTH, W_B)),
        'w_out': nrm(ks[18], (DEPTH, MIX_WIDTH, D_MODEL), MIX_WIDTH ** -0.5),
        'norm_ffn': gain(ks[19], (DEPTH, D_MODEL)),
        'w_up': nrm(ks[20], (DEPTH, D_MODEL, 2 * D_FF), D_MODEL ** -0.5),
        'conv_w': nrm(ks[21], (DEPTH, CONV_W, D_FF), CONV_W ** -0.5),
        'conv_b': nrm(ks[22], (DEPTH, D_FF), 0.02),
        'w_down': nrm(ks[23], (DEPTH, D_FF, D_MODEL), D_FF ** -0.5),
        'norm_ple': gain(ks[24], (DEPTH, D_MODEL)),
        'w_ple_gate': nrm(ks[25], (DEPTH, D_MODEL, D_MODEL), D_MODEL ** -0.5),
        'w_ple_proj': nrm(ks[26], (DEPTH, PLE_DIM, D_MODEL), PLE_DIM ** -0.5),
    }


def reference(x_prompt, x_sample, cache_win_k, cache_win_v, state_C, state_n, state_m, state_conv,
              p_prompt, p_sample, norm_mix, w_in, q_norm, k_norm, b_gates, out_norm_a, out_norm_b,
              w_out, norm_ffn, w_up, conv_w, conv_b, w_down, norm_ple, w_ple_gate, w_ple_proj):
    seq = x_prompt.shape[1]
    dec_seq = x_sample.shape[1]
    buf_len = cache_win_k.shape[2]
    keep_p = min(WIN_MAX, seq)
    pos_p = jnp.arange(seq, dtype=jnp.int32)
    pos_s = PAST_LEN + jnp.arange(dec_seq, dtype=jnp.int32)
    xp, xs = x_prompt, x_sample
    wk_p, wv_p, C_p, n_p, m_p, cv_p = [], [], [], [], [], []
    wk_s, wv_s, C_s, n_s, m_s, cv_s = [], [], [], [], [], []
    for i in range(DEPTH):
        qa, ka, va, qb, kb, vb, ig, lf, og = mixer_inputs(rmsnorm(xp, norm_mix[i]), w_in[i], q_norm[i],
                                                          k_norm[i], b_gates[i], pos_p)
        o_a = combine_by_denominators([dilated_band_stats(qa, ka, va, w, d) for (w, d) in DILATED_PATTERNS])
        h_b, (c_new, n_new, m_new) = mlstm_prompt(qb, kb, vb, ig, lf)
        xp = merge_heads(xp, o_a, h_b, og, out_norm_a[i], out_norm_b[i], w_out[i])
        conv0 = jnp.zeros((xp.shape[0], CONV_W - 1, D_FF), xp.dtype)
        xp, conv_new = conv_glu(xp, conv0, norm_ffn[i], w_up[i], conv_w[i], conv_b[i], w_down[i])
        xp = per_layer_embed(xp, p_prompt[i], norm_ple[i], w_ple_gate[i], w_ple_proj[i])
        wk_p.append(ka[:, seq - keep_p:])
        wv_p.append(va[:, seq - keep_p:])
        C_p.append(c_new)
        n_p.append(n_new)
        m_p.append(m_new)
        cv_p.append(conv_new)
        qa, ka, va, qb, kb, vb, ig, lf, og = mixer_inputs(rmsnorm(xs, norm_mix[i]), w_in[i], q_norm[i],
                                                          k_norm[i], b_gates[i], pos_s)
        k_all = jnp.concatenate([cache_win_k[i].astype(ka.dtype), ka], axis=1)
        v_all = jnp.concatenate([cache_win_v[i].astype(va.dtype), va], axis=1)
        o_a = combine_by_denominators([dilated_gather_stats(qa, k_all, v_all, w, d, buf_len)
                                       for (w, d) in DILATED_PATTERNS])
        carry0 = (state_C[i].astype(jnp.float32), state_n[i].astype(jnp.float32), state_m[i].astype(jnp.float32))
        (c_new, n_new, m_new), h_b = mlstm_chunk(carry0, (qb, kb, vb, ig, lf))
        xs = merge_heads(xs, o_a, h_b, og, out_norm_a[i], out_norm_b[i], w_out[i])
        xs, conv_new = conv_glu(xs, state_conv[i], norm_ffn[i], w_up[i], conv_w[i], conv_b[i], w_down[i])
        xs = per_layer_embed(xs, p_sample[i], norm_ple[i], w_ple_gate[i], w_ple_proj[i])
        wk_s.append(ka)
        wv_s.append(va)
        C_s.append(c_new)
        n_s.append(n_new)
        m_s.append(m_new)
        cv_s.append(conv_new)
    st = jnp.stack
    return (xp, xs, st(wk_p), st(wv_p), st(C_p), st(n_p), st(m_p), st(cv_p),
            st(wk_s), st(wv_s), st(C_s), st(n_s), st(m_s), st(cv_s))
```

```python
import functools

import jax
import jax.numpy as jnp
from jax import lax
from jax.experimental import pallas as pl
from jax.experimental.pallas import tpu as pltpu

F32 = jnp.float32
BF16 = jnp.bfloat16

HEAD_DIM = 64
ROT_DIM = HEAD_DIM // 4
ROPE_THETA = 500000.0
RMS_EPS = 1e-6
NEG_INF = -1e30
PAST_LEN = 8192
DILATIONS = (1, 4, 16)
BAND = 128
CHUNK = 128
LANES = 128
GATE_LANE0 = LANES - 16
VMEM_LIMIT = 56 * 1024 * 1024


def _cparams(n_axes):
    return pltpu.CompilerParams(dimension_semantics=("arbitrary",) * n_axes,
                                vmem_limit_bytes=VMEM_LIMIT)


def _const_spec(shape):
    return pl.BlockSpec(shape, lambda *_: (0,) * len(shape))


def _rms(x, g):
    return (x * lax.rsqrt(jnp.mean(x * x, axis=-1, keepdims=True) + RMS_EPS)) * g


def _dot(a, b):
    return jnp.dot(a, b, preferred_element_type=F32)


def _dot_nt(a, b):
    return lax.dot_general(a, b, (((1,), (1,)), ((), ())), preferred_element_type=F32)


def _dot_tn(a, b):
    return lax.dot_general(a, b, (((0,), (0,)), ((), ())), preferred_element_type=F32)


def _split_dot(a, b):
    hi = a.astype(BF16)
    lo = (a - hi.astype(F32)).astype(BF16)
    return _dot(hi, b) + _dot(lo, b)


def _log_sigmoid(x):
    return jnp.minimum(x, 0.0) - jnp.log1p(jnp.exp(-jnp.abs(x)))


def _gelu_tanh(x):
    c = 0.7978845608028654
    return x * (0.5 * (1.0 + jnp.tanh(c * (x + 0.044715 * (x * x * x)))))


def _inproj_body(x_ref, g_ref, w_ref, wg_ref, qn_ref, kn_ref, bg_ref, cos_ref, sa_ref, sb_ref,
                 ind_ref, indt_ref,
                 qa_ref, ka_ref, va_ref, qb_ref, kb_ref, vb_ref, og_ref, gt_ref, *, wa, transpose_gates):
    h = _rms(x_ref[...], g_ref[...]).astype(BF16)
    reps = wa // LANES
    cos = jnp.concatenate([cos_ref[...]] * reps, axis=1)
    sa = jnp.concatenate([sa_ref[...]] * reps, axis=1)
    sb = jnp.concatenate([sb_ref[...]] * reps, axis=1)
    ind = ind_ref[...]
    indt = indt_ref[...]

    def head_norm_rope(z, gain):
        ss = _split_dot(z * z, ind)
        inv = lax.rsqrt(ss * (1.0 / HEAD_DIM) + RMS_EPS)
        zn = (z * _split_dot(inv, indt)) * gain
        return zn * cos + pltpu.roll(zn, wa - ROT_DIM // 2, 1) * sa + pltpu.roll(zn, ROT_DIM // 2, 1) * sb

    qa_ref[...] = head_norm_rope(_dot(h, w_ref[:, 0:wa]), qn_ref[...]) * (HEAD_DIM ** -0.5)
    ka_ref[...] = head_norm_rope(_dot(h, w_ref[:, wa:2 * wa]), kn_ref[...])
    va_ref[...] = _dot(h, w_ref[:, 2 * wa:3 * wa])
    wb = (w_ref.shape[1] - 3 * wa) // 4
    o = 3 * wa
    qb_ref[...] = _dot(h, w_ref[:, o:o + wb]).astype(qb_ref.dtype)
    kb_ref[...] = (_dot(h, w_ref[:, o + wb:o + 2 * wb]) * (HEAD_DIM ** -0.5)).astype(kb_ref.dtype)
    vb_ref[...] = _dot(h, w_ref[:, o + 2 * wb:o + 3 * wb]).astype(vb_ref.dtype)
    og_ref[...] = jax.nn.sigmoid(_dot(h, w_ref[:, o + 3 * wb:o + 4 * wb]))

    v = _dot(h, wg_ref[...]) + bg_ref[...]
    lane = lax.broadcasted_iota(jnp.int32, v.shape, 1)
    ga = jnp.where(lane >= GATE_LANE0 + 8, _log_sigmoid(v), v)
    if transpose_gates:
        gt_ref[0] = ga.T[GATE_LANE0:, :]
    else:
        gt_ref[...] = ga


def _inproj(x, gain, w_main, w_gate, qn, kn, bg, cos, sa, sb, ind, indt, *, tm, seq_tiles, batch, mdtype,
            transpose_gates):
    t, d = x.shape
    wa = qn.shape[1]
    wb = (w_main.shape[1] - 3 * wa) // 4
    nt = t // tm
    tok = lambda w: pl.BlockSpec((tm, w), lambda i: (i, 0))
    tab = pl.BlockSpec((tm, LANES), lambda i: (i % seq_tiles, 0))
    if transpose_gates:
        seq = seq_tiles * tm
        gt_shape = jax.ShapeDtypeStruct((batch, 16, seq), F32)
        gt_spec = pl.BlockSpec((1, 16, tm), lambda i: (i // seq_tiles, 0, i % seq_tiles))
    else:
        gt_shape = jax.ShapeDtypeStruct((t, LANES), F32)
        gt_spec = tok(LANES)
    out_shape = (
        jax.ShapeDtypeStruct((t, wa), F32), jax.ShapeDtypeStruct((t, wa), F32), jax.ShapeDtypeStruct((t, wa), F32),
        jax.ShapeDtypeStruct((t, wb), mdtype), jax.ShapeDtypeStruct((t, wb), mdtype),
        jax.ShapeDtypeStruct((t, wb), mdtype), jax.ShapeDtypeStruct((t, wb), F32), gt_shape)
    return pl.pallas_call(
        functools.partial(_inproj_body, wa=wa, transpose_gates=transpose_gates),
        grid=(nt,),
        in_specs=[tok(d), _const_spec(gain.shape), _const_spec(w_main.shape), _const_spec(w_gate.shape),
                  _const_spec(qn.shape), _const_spec(kn.shape), _const_spec(bg.shape), tab, tab, tab,
                  _const_spec(ind.shape), _const_spec(indt.shape)],
        out_specs=(tok(wa), tok(wa), tok(wa), tok(wb), tok(wb), tok(wb), tok(wb), gt_spec),
        out_shape=out_shape,
        compiler_params=_cparams(1),
        name="inproj",
    )(x, gain, w_main, w_gate, qn, kn, bg, cos, sa, sb, ind, indt)


def _attn_body(q_ref, k_ref, v_ref, o_ref, acc_ref, m0_ref, m1_ref, l0_ref, l1_ref, *, seq):
    qi = lax.broadcasted_iota(jnp.int32, (BAND, 2 * BAND), 0)
    kj = lax.broadcasted_iota(jnp.int32, (BAND, 2 * BAND), 1)
    band = (kj >= qi) & (kj <= qi + BAND)
    in_cur = kj >= BAND
    lane = lax.broadcasted_iota(jnp.int32, (BAND, LANES), 1)
    head0 = lane < HEAD_DIM
    m_refs = (m0_ref, m1_ref)
    l_refs = (l0_ref, l1_ref)

    for pi, dil in enumerate(DILATIONS):
        nb = seq // (dil * BAND)
        first, last = pi == 0, pi == len(DILATIONS) - 1

        def block(idx, carry, dil=dil, nb=nb, first=first, last=last):
            r = idx // nb
            n = idx - r * nb
            start = r + n * (BAND * dil)
            prev = jnp.maximum(start - BAND * dil, r)
            rows = pl.ds(start, BAND, stride=dil)
            prows = pl.ds(prev, BAND, stride=dil)
            q = q_ref[0, rows, :]
            kk = jnp.concatenate([k_ref[0, prows, :], k_ref[0, rows, :]], axis=0).astype(BF16)
            vv = jnp.concatenate([v_ref[0, prows, :], v_ref[0, rows, :]], axis=0).astype(BF16)
            mask = band & (in_cur | (n > 0))
            qh = (jnp.where(head0, q, 0.0).astype(BF16), jnp.where(head0, 0.0, q).astype(BF16))
            pv, alpha, lnew = [], [], []
            for h in range(2):
                s = jnp.where(mask, _dot_nt(qh[h], kk), NEG_INF)
                rm = jnp.max(s, axis=1, keepdims=True)
                if first:
                    mn = jnp.broadcast_to(rm, (BAND, LANES))
                else:
                    mo = m_refs[h][rows, :]
                    mn = jnp.maximum(mo, rm)
                p = jnp.exp(s - jnp.concatenate([mn, mn], axis=1))
                ls = jnp.sum(p, axis=1, keepdims=True)
                pv.append(_dot(p.astype(BF16), vv))
                if first:
                    lnew.append(jnp.broadcast_to(ls, (BAND, LANES)))
                else:
                    a = jnp.exp(mo - mn)
                    alpha.append(a)
                    lnew.append(a * l_refs[h][rows, :] + ls)
                if not last:
                    m_refs[h][rows, :] = mn
                    l_refs[h][rows, :] = lnew[h]
            acc = jnp.where(head0, pv[0], pv[1])
            if not first:
                acc = jnp.where(head0, alpha[0], alpha[1]) * acc_ref[rows, :] + acc
            if last:
                o_ref[0, rows, :] = acc / jnp.where(head0, lnew[0], lnew[1])
            else:
                acc_ref[rows, :] = acc
            return carry

        lax.fori_loop(0, dil * nb, block, 0)


def _attention(q, k, v):
    b, s, wa = q.shape
    spec = pl.BlockSpec((1, s, LANES), lambda i, j: (i, 0, j))
    scratch = [pltpu.VMEM((s, LANES), F32) for _ in range(5)]
    return pl.pallas_call(
        functools.partial(_attn_body, seq=s),
        grid=(b, wa // LANES),
        in_specs=[spec, spec, spec],
        out_specs=spec,
        out_shape=jax.ShapeDtypeStruct((b, s, wa), F32),
        scratch_shapes=scratch,
        compiler_params=_cparams(2),
        name="dilated_attn",
    )(q, k, v)


def _mlstm_body(q_ref, k_ref, v_ref, g_ref, h_ref, c_ref, n_ref, m_ref, cst_ref, nst_ref, mst_ref, *, nchunks, nheads):
    hp = pl.program_id(1)
    L = CHUNK
    row = lax.broadcasted_iota(jnp.int32, (L, L), 0)
    lane = lax.broadcasted_iota(jnp.int32, (L, L), 1)
    lane1 = lax.broadcasted_iota(jnp.int32, (1, L), 1)
    head_row = lax.broadcasted_iota(jnp.int32, (nheads, L), 0)
    causal = lane <= row
    diag = lane == row
    head0 = lane < HEAD_DIM
    blockdiag = (row < HEAD_DIM) == head0
    ones = jnp.ones((L, L), BF16)

    cst_ref[...] = jnp.zeros_like(cst_ref)
    nst_ref[...] = jnp.zeros_like(nst_ref)
    mst_ref[...] = jnp.zeros_like(mst_ref)

    def to_col(r):
        return jnp.sum(jnp.where(diag, r, 0.0), axis=1, keepdims=True)

    def chunk(c, carry):
        t0 = pl.multiple_of(c * L, L)
        q = q_ref[0, pl.ds(t0, L), :]
        k = k_ref[0, pl.ds(t0, L), :]
        v = v_ref[0, pl.ds(t0, L), :]
        zero = jnp.zeros_like(q)
        qq = jnp.concatenate([jnp.where(head0, q, zero), jnp.where(head0, zero, q)], axis=0)
        qk = _dot_nt(qq, k)
        cst = cst_ref[...]
        nst = nst_ref[...]
        qcn = _dot(q, jnp.concatenate([cst.astype(BF16), nst.astype(BF16)], axis=1))

        sc, iw, rs, en, ws, cw, mnew = [], [], [], [], [], [], []
        ig_all = g_ref[0, 0:nheads, pl.ds(t0, L)]
        lf_all = g_ref[0, nheads:2 * nheads, pl.ds(t0, L)]
        for h in range(2):
            pick = head_row == 2 * hp + h
            ig = jnp.sum(jnp.where(pick, ig_all, 0.0), axis=0, keepdims=True)
            lf = jnp.sum(jnp.where(pick, lf_all, 0.0), axis=0, keepdims=True)
            b_row = lf
            sh = 1
            while sh < L:
                b_row = b_row + jnp.where(lane1 >= sh, pltpu.roll(b_row, sh, 1), 0.0)
                sh *= 2
            g_row = ig - b_row
            bc = to_col(b_row)
            m_prev = mst_ref[h:h + 1, 0:1]
            d = jnp.where(causal, bc + g_row, NEG_INF)
            inter = bc + m_prev
            m_t = jnp.maximum(inter, jnp.max(d, axis=1, keepdims=True))
            dw = jnp.exp(d - m_t)
            s_h = qk[h * L:(h + 1) * L, :] * dw
            sc.append(s_h)
            iw.append(jnp.exp(inter - m_t))
            rs.append(jnp.sum(s_h, axis=1, keepdims=True))
            en.append(jnp.exp(-m_t))
            m_new = m_t[L - 1:L, :]
            b_last = bc[L - 1:L, :]
            ws.append(to_col(jnp.exp((b_last + g_row) - m_new)))
            cw.append(jnp.exp((b_last + m_prev) - m_new))
            mnew.append(m_new)

        scv = _dot(jnp.concatenate(sc, axis=0).astype(BF16), v)
        intra = jnp.where(head0, scv[:L], scv[L:])
        iw_l = jnp.where(head0, iw[0], iw[1])
        num = iw_l * qcn[:, :L] + intra
        den = iw_l * qcn[:, L:] + jnp.where(head0, rs[0], rs[1])
        h_ref[0, pl.ds(t0, L), :] = num / jnp.maximum(jnp.abs(den), jnp.where(head0, en[0], en[1]))

        kw = (k.astype(F32) * jnp.where(head0, ws[0], ws[1])).astype(BF16)
        u = _dot_tn(kw, jnp.concatenate([v, ones], axis=1))
        cw_l = jnp.where(head0, cw[0], cw[1])
        cst_ref[...] = cw_l * cst + jnp.where(blockdiag, u[:, :L], 0.0)
        nst_ref[...] = cw_l * nst + jnp.where(blockdiag, u[:, L:], 0.0)
        mst_ref[...] = jnp.concatenate([jnp.broadcast_to(mnew[0], (1, L)), jnp.broadcast_to(mnew[1], (1, L))], axis=0)
        return carry

    lax.fori_loop(0, nchunks, chunk, 0)

    ct = cst_ref[...].T
    nt = nst_ref[...].T
    for h in range(2):
        sl = slice(h * HEAD_DIM, (h + 1) * HEAD_DIM)
        c_ref[0, h] = ct[sl, sl]
        n_ref[0, 0, h:h + 1, :] = nt[h * HEAD_DIM:h * HEAD_DIM + 1, sl]
    m_ref[0, 0] = mst_ref[...]


def _mlstm(q, k, v, gates):
    b, s, wb = q.shape
    nheads = wb // HEAD_DIM
    npairs = wb // LANES
    spec = pl.BlockSpec((1, s, LANES), lambda i, j: (i, 0, j))
    out_shape = (jax.ShapeDtypeStruct((b, s, wb), F32),
                 jax.ShapeDtypeStruct((b, nheads, HEAD_DIM, HEAD_DIM), F32),
                 jax.ShapeDtypeStruct((b, npairs, 2, HEAD_DIM), F32),
                 jax.ShapeDtypeStruct((b, npairs, 2, LANES), F32))
    out_specs = (spec,
                 pl.BlockSpec((1, 2, HEAD_DIM, HEAD_DIM), lambda i, j: (i, j, 0, 0)),
                 pl.BlockSpec((1, 1, 2, HEAD_DIM), lambda i, j: (i, j, 0, 0)),
                 pl.BlockSpec((1, 1, 2, LANES), lambda i, j: (i, j, 0, 0)))
    return pl.pallas_call(
        functools.partial(_mlstm_body, nchunks=s // CHUNK, nheads=nheads),
        grid=(b, npairs),
        in_specs=[spec, spec, spec, pl.BlockSpec((1, 2 * nheads, s), lambda i, j: (i, 0, 0))],
        out_specs=out_specs,
        out_shape=out_shape,
        scratch_shapes=[pltpu.VMEM((LANES, LANES), F32), pltpu.VMEM((LANES, LANES), F32), pltpu.VMEM((2, LANES), F32)],
        compiler_params=_cparams(2),
        name="mlstm",
    )(q, k, v, gates)


def _merge_body(x_ref, oa_ref, hb_ref, og_ref, ga_ref, gb_ref, w_ref, o_ref):
    wa = oa_ref.shape[1]
    ya = _rms(oa_ref[...], ga_ref[...]).astype(BF16)
    yb = (og_ref[...] * _rms(hb_ref[...], gb_ref[...])).astype(BF16)
    o_ref[...] = x_ref[...] + _dot(ya, w_ref[0:wa, :]) + _dot(yb, w_ref[wa:, :])


def _merge(x, oa, hb, og, ga, gb, w, *, tm):
    t, d = x.shape
    tok = lambda w_: pl.BlockSpec((tm, w_), lambda i: (i, 0))
    return pl.pallas_call(
        _merge_body,
        grid=(t // tm,),
        in_specs=[tok(d), tok(oa.shape[1]), tok(hb.shape[1]), tok(og.shape[1]),
                  _const_spec(ga.shape), _const_spec(gb.shape), _const_spec(w.shape)],
        out_specs=tok(d),
        out_shape=jax.ShapeDtypeStruct((t, d), F32),
        compiler_params=_cparams(1),
        name="merge_out",
    )(x, oa, hb, og, ga, gb, w)


def _ffn_body(*refs, prompt, seq_tiles, ck):
    if prompt:
        x_ref, g_ref, wup_ref, cw_ref, cb_ref, wdn_ref, o_ref, conv_ref, y_ref, carry_ref = refs
    else:
        x_ref, g_ref, wup_ref, cw_ref, cb_ref, wdn_ref, p0_ref, p1_ref, o_ref, conv_ref, y_ref = refs
    tm = x_ref.shape[0]
    dff = wdn_ref.shape[0]
    x = x_ref[...]
    h = _rms(x, g_ref[...]).astype(BF16)
    if prompt:
        @pl.when(pl.program_id(0) % seq_tiles == 0)
        def _():
            carry_ref[...] = jnp.zeros_like(carry_ref)
        row = lax.broadcasted_iota(jnp.int32, (tm, ck), 0)
    for c in range(dff // ck):
        cs = slice(c * ck, (c + 1) * ck)
        gate = _dot(h, wup_ref[:, c * ck:(c + 1) * ck])
        val = _dot(h, wup_ref[:, dff + c * ck:dff + (c + 1) * ck])
        if prompt:
            b0 = carry_ref[0:1, cs]
            b1 = carry_ref[1:2, cs]
            g1 = jnp.where(row == 0, b1, pltpu.roll(gate, 1, 0))
            g2 = jnp.where(row == 0, b0, jnp.where(row == 1, b1, pltpu.roll(gate, 2, 0)))
            carry_ref[:, cs] = gate[tm - 2:tm, :]
        else:
            g2 = p0_ref[:, cs]
            g1 = p1_ref[:, cs]
            conv_ref[:, cs] = gate
        gc = cb_ref[:, cs] + (cw_ref[0:1, cs] * g2 + cw_ref[1:2, cs] * g1 + cw_ref[2:3, cs] * gate)
        y_ref[:, cs] = (_gelu_tanh(gc) * val).astype(BF16)
    if prompt:
        conv_ref[0] = carry_ref[...]
    o_ref[...] = x + _dot(y_ref[...], wdn_ref[...])


def _ffn(x, gain, w_up, conv_w, conv_b, w_down, prev=None, *, tm, seq_tiles=1, batch=1):
    t, d = x.shape
    dff = w_down.shape[0]
    prompt = prev is None
    tok = lambda w_: pl.BlockSpec((tm, w_), lambda i: (i, 0))
    in_specs = [tok(d), _const_spec(gain.shape), _const_spec(w_up.shape), _const_spec(conv_w.shape),
                _const_spec(conv_b.shape), _const_spec(w_down.shape)]
    args = [x, gain, w_up, conv_w, conv_b, w_down]
    scratch = [pltpu.VMEM((tm, dff), BF16)]
    if prompt:
        conv_shape = jax.ShapeDtypeStruct((batch, 2, dff), F32)
        conv_spec = pl.BlockSpec((1, 2, dff), lambda i: (i // seq_tiles, 0, 0))
        scratch.append(pltpu.VMEM((2, dff), F32))
    else:
        in_specs += [tok(dff), tok(dff)]
        args += list(prev)
        conv_shape = jax.ShapeDtypeStruct((t, dff), F32)
        conv_spec = tok(dff)
    return pl.pallas_call(
        functools.partial(_ffn_body, prompt=prompt, seq_tiles=seq_tiles, ck=256),
        grid=(t // tm,),
        in_specs=in_specs,
        out_specs=(tok(d), conv_spec),
        out_shape=(jax.ShapeDtypeStruct((t, d), F32), conv_shape),
        scratch_shapes=scratch,
        compiler_params=_cparams(1),
        name="conv_glu",
    )(*args)


def _ple_body(x_ref, p_ref, g_ref, wg_ref, wp_ref, o_ref):
    x = x_ref[...]
    gate = jax.nn.sigmoid(_dot(_rms(x, g_ref[...]).astype(BF16), wg_ref[...]))
    o_ref[...] = x + gate * _dot(p_ref[...].astype(BF16), wp_ref[...])


def _ple(x, p, gain, w_gate, w_proj, *, tm):
    t, d = x.shape
    tok = lambda w_: pl.BlockSpec((tm, w_), lambda i: (i, 0))
    return pl.pallas_call(
        _ple_body,
        grid=(t // tm,),
        in_specs=[tok(d), tok(p.shape[1]), _const_spec(gain.shape), _const_spec(w_gate.shape),
                  _const_spec(w_proj.shape)],
        out_specs=tok(d),
        out_shape=jax.ShapeDtypeStruct((t, d), F32),
        compiler_params=_cparams(1),
        name="ple",
    )(x, p, gain, w_gate, w_proj)


def _sattn_body(q_ref, kn_ref, vn_ref, k1_ref, v1_ref, k4_ref, v4_ref, k16_ref, v16_ref, o_ref):
    q = q_ref[0]
    kn = kn_ref[0]
    vn = vn_ref[0]
    s_new = jnp.sum(q * kn, axis=-1, keepdims=True)
    stats = []
    for k_ref, v_ref in ((k1_ref, v1_ref), (k4_ref, v4_ref), (k16_ref, v16_ref)):
        kc = k_ref[...]
        vc = v_ref[...]
        s = jnp.sum(kc * q[None], axis=-1, keepdims=True)
        m = jnp.maximum(jnp.max(s, axis=0), s_new)
        e = jnp.exp(s - m[None])
        e_new = jnp.exp(s_new - m)
        l = jnp.sum(e, axis=0) + e_new
        o = (jnp.sum(e * vc, axis=0) + e_new * vn) / l
        stats.append((m, l, o))
    mm = jnp.maximum(jnp.maximum(stats[0][0], stats[1][0]), stats[2][0])
    ws = [l * jnp.exp(m - mm) for m, l, _ in stats]
    num = ws[0] * stats[0][2] + ws[1] * stats[1][2] + ws[2] * stats[2][2]
    o_ref[0] = num / (ws[0] + ws[1] + ws[2])


def _sample_attention(q, kn, vn, ck, cv, layer):
    b, nh, dh = q.shape
    depth, _, wbuf = ck.shape[:3]
    row = pl.BlockSpec((1, nh, dh), lambda i: (i, 0, 0))
    views, specs = [], []
    for dil in DILATIONS:
        groups = wbuf // dil
        gblk = groups // BAND - 1
        shape = (depth, b, groups, dil, nh, dh)
        spec = pl.BlockSpec((None, None, BAND, None, nh, dh),
                            lambda i, gblk=gblk: (layer, i, gblk, 0, 0, 0))
        for c in (ck, cv):
            views.append(c.reshape(shape))
            specs.append(spec)
    return pl.pallas_call(
        _sattn_body,
        grid=(b,),
        in_specs=[row, row, row] + specs,
        out_specs=row,
        out_shape=jax.ShapeDtypeStruct((b, nh, dh), F32),
        compiler_params=_cparams(1),
        name="sample_attn",
    )(q, kn, vn, *views)


def _smlstm_body(q_ref, k_ref, v_ref, ig_ref, lf_ref, c_ref, n_ref, m_ref, h_ref, co_ref, no_ref, mo_ref):
    q, k, v = q_ref[0], k_ref[0], v_ref[0]
    ig, lf, m = ig_ref[0], lf_ref[0], m_ref[0]
    n = n_ref[0]
    nh = q.shape[0]
    inter = lf + m
    m_t = jnp.maximum(inter, ig)
    dw = jnp.exp(ig - m_t)
    iw = jnp.exp(inter - m_t)
    sc = jnp.sum(q * k, axis=-1, keepdims=True) * dw
    r = lax.broadcasted_iota(jnp.int32, (HEAD_DIM, HEAD_DIM), 0)
    l = lax.broadcasted_iota(jnp.int32, (HEAD_DIM, HEAD_DIM), 1)
    cq = []
    for h in range(nh):
        c_h = c_ref[0, h]
        q_h = jnp.broadcast_to(q[h:h + 1, :], (8, HEAD_DIM)).astype(BF16)
        cq.append(_dot_nt(q_h, c_h.astype(BF16))[0:1, :])
        v_col = jnp.sum(jnp.where(r == l, v[h:h + 1, :], 0.0), axis=1, keepdims=True)
        co_ref[0, h] = iw[h:h + 1, :] * c_h + (dw[h:h + 1, :] * v_col) * k[h:h + 1, :]
    num = iw * jnp.concatenate(cq, axis=0) + sc * v
    den = iw * jnp.sum(n * q, axis=-1, keepdims=True) + sc
    h_ref[0] = num / jnp.maximum(jnp.abs(den), jnp.exp(-m_t))
    no_ref[0] = iw * n + dw * k
    mo_ref[0] = m_t


def _sample_mlstm(q, k, v, ig, lf, c, n, m):
    b, nh, dh = q.shape
    row = pl.BlockSpec((1, nh, dh), lambda i: (i, 0, 0))
    col = pl.BlockSpec((1, nh, 1), lambda i: (i, 0, 0))
    cst = pl.BlockSpec((1, nh, dh, dh), lambda i: (i, 0, 0, 0))
    return pl.pallas_call(
        _smlstm_body,
        grid=(b,),
        in_specs=[row, row, row, col, col, cst, row, col],
        out_specs=(row, cst, row, col),
        out_shape=(jax.ShapeDtypeStruct((b, nh, dh), F32), jax.ShapeDtypeStruct((b, nh, dh, dh), F32),
                   jax.ShapeDtypeStruct((b, nh, dh), F32), jax.ShapeDtypeStruct((b, nh, 1), F32)),
        compiler_params=_cparams(1),
        name="sample_mlstm",
    )(q, k, v, ig, lf, c, n, m)


def _rope_tables(pos):
    half = ROT_DIM // 2
    inv = ROPE_THETA ** (-jnp.arange(half, dtype=F32) / half)
    ang = pos.astype(F32)[:, None] * inv[None, :]
    cos, sin = jnp.cos(ang), jnp.sin(ang)
    n = pos.shape[0]
    rest = HEAD_DIM - ROT_DIM
    c = jnp.concatenate([cos, cos, jnp.ones((n, rest), F32)], axis=1)
    sa = jnp.concatenate([-sin, jnp.zeros((n, half + rest), F32)], axis=1)
    sb = jnp.concatenate([jnp.zeros((n, half), F32), sin, jnp.zeros((n, rest), F32)], axis=1)
    rep = LANES // HEAD_DIM
    return tuple(jnp.concatenate([t] * rep, axis=1) for t in (c, sa, sb))


def kernel(x_prompt, x_sample, cache_win_k, cache_win_v, state_C, state_n, state_m, state_conv, p_prompt, p_sample,
           norm_mix, w_in, q_norm, k_norm, b_gates, out_norm_a, out_norm_b, w_out, norm_ffn, w_up, conv_w, conv_b,
           w_down, norm_ple, w_ple_gate, w_ple_proj):
    batch, seq, d = x_prompt.shape
    dec_batch, dec_seq, _ = x_sample.shape
    depth = w_in.shape[0]
    nh_a = cache_win_k.shape[3]
    nh_b = state_C.shape[2]
    wa, wb = nh_a * HEAD_DIM, nh_b * HEAD_DIM
    wbuf = cache_win_k.shape[2]
    win_max = BAND * DILATIONS[-1]
    keep = min(win_max, seq)
    assert dec_seq == 1 and wbuf == win_max and seq % win_max == 0
    assert w_in.shape[2] == 3 * wa + 4 * wb + 2 * nh_b and wa % LANES == 0 and wb % LANES == 0

    tm = 512
    seq_tiles = seq // tm
    tp, ts = batch * seq, dec_batch * dec_seq

    cut = 3 * wa + 4 * wb
    w_main = w_in[:, :, :cut].astype(BF16)
    w_gate = w_in[:, :, cut + 2 * nh_b - LANES:].astype(BF16)
    bg = jnp.concatenate([jnp.zeros((depth, GATE_LANE0), F32), b_gates], axis=1)[:, None, :]
    qn = jnp.tile(q_norm, (1, nh_a))[:, None, :]
    kn = jnp.tile(k_norm, (1, nh_a))[:, None, :]
    w_out_b, w_up_b, w_down_b = w_out.astype(BF16), w_up.astype(BF16), w_down.astype(BF16)
    w_pg_b, w_pp_b = w_ple_gate.astype(BF16), w_ple_proj.astype(BF16)
    head_of_lane = jnp.arange(wa) // HEAD_DIM
    ind = (head_of_lane[:, None] == jnp.arange(LANES)[None, :]).astype(BF16)
    indt = ind.T
    tab_p = _rope_tables(jnp.arange(seq, dtype=jnp.int32))
    tab_s = tuple(jnp.broadcast_to(t, (ts, LANES))
                  for t in _rope_tables(PAST_LEN + jnp.arange(dec_seq, dtype=jnp.int32)))

    xp = x_prompt.reshape(tp, d)
    xs = x_sample.reshape(ts, d)
    outs = [[] for _ in range(12)]
    for i in range(depth):
        row = lambda a: a[i][None, :]
        qa, ka, va, qb, kb, vb, og, gt = _inproj(
            xp, row(norm_mix), w_main[i], w_gate[i], qn[i], kn[i], bg[i], *tab_p, ind, indt,
            tm=tm, seq_tiles=seq_tiles, batch=batch, mdtype=BF16, transpose_gates=True)
        r3 = lambda a: a.reshape(batch, seq, a.shape[-1])
        oa = _attention(r3(qa), r3(ka), r3(va))
        hb, c_new, n_new, m_new = _mlstm(r3(qb), r3(kb), r3(vb), gt)
        xp = _merge(xp, oa.reshape(tp, wa), hb.reshape(tp, wb), og, row(out_norm_a), row(out_norm_b), w_out_b[i], tm=tm)
        xp, conv_new = _ffn(xp, row(norm_ffn), w_up_b[i], conv_w[i], row(conv_b), w_down_b[i],
                            tm=tm, seq_tiles=seq_tiles, batch=batch)
        xp = _ple(xp, p_prompt[i].reshape(tp, -1), row(norm_ple), w_pg_b[i], w_pp_b[i], tm=tm)
        outs[0].append(ka.reshape(batch, seq, nh_a, HEAD_DIM)[:, seq - keep:])
        outs[1].append(va.reshape(batch, seq, nh_a, HEAD_DIM)[:, seq - keep:])
        outs[2].append(c_new)
        outs[3].append(n_new.reshape(batch, nh_b, HEAD_DIM))
        outs[4].append(m_new[..., 0].reshape(batch, nh_b))
        outs[5].append(conv_new)
        qa, ka, va, qb, kb, vb, og, ga = _inproj(
            xs, row(norm_mix), w_main[i], w_gate[i], qn[i], kn[i], bg[i], *tab_s, ind, indt,
            tm=ts, seq_tiles=1, batch=dec_batch, mdtype=F32, transpose_gates=False)
        hd = lambda a: a.reshape(dec_batch, -1, HEAD_DIM)
        oa = _sample_attention(hd(qa), hd(ka), hd(va), cache_win_k, cache_win_v, i)
        ig = ga[:, GATE_LANE0:GATE_LANE0 + nh_b, None]
        lf = ga[:, GATE_LANE0 + nh_b:, None]
        hb, c_new, n_new, m_new = _sample_mlstm(hd(qb), hd(kb), hd(vb), ig, lf, state_C[i], state_n[i],
                                                state_m[i][..., None])
        xs = _merge(xs, oa.reshape(ts, wa), hb.reshape(ts, wb), og, row(out_norm_a), row(out_norm_b), w_out_b[i], tm=ts)
        xs, gate_new = _ffn(xs, row(norm_ffn), w_up_b[i], conv_w[i], row(conv_b), w_down_b[i],
                            prev=(state_conv[i][:, 0], state_conv[i][:, 1]), tm=ts)
        xs = _ple(xs, p_sample[i].reshape(ts, -1), row(norm_ple), w_pg_b[i], w_pp_b[i], tm=ts)
        outs[6].append(ka.reshape(dec_batch, dec_seq, nh_a, HEAD_DIM))
        outs[7].append(va.reshape(dec_batch, dec_seq, nh_a, HEAD_DIM))
        outs[8].append(c_new)
        outs[9].append(n_new)
        outs[10].append(m_new[..., 0])
        outs[11].append(jnp.stack([state_conv[i][:, 1], gate_new], axis=1))
    st = jnp.stack
    return (xp.reshape(batch, seq, d), xs.reshape(dec_batch, dec_seq, d)) + tuple(st(o) for o in outs)
```

```python
import functools

import jax
import jax.numpy as jnp
from jax import lax
from jax.experimental import pallas as pl
from jax.experimental.pallas import tpu as pltpu

F32 = jnp.float32
BF16 = jnp.bfloat16

HEAD_DIM = 64
ROT_DIM = HEAD_DIM // 4
ROPE_THETA = 500000.0
RMS_EPS = 1e-6
NEG_INF = -1e30
PAST_LEN = 8192
DILATIONS = (1, 4, 16)
BAND = 128
ATTN_GROUP = 4
CHUNK = 128
LANES = 128
GATE_LANE0 = LANES - 16
VMEM_LIMIT = 56 * 1024 * 1024


def _cparams(n_axes):
    return pltpu.CompilerParams(dimension_semantics=("arbitrary",) * n_axes,
                                vmem_limit_bytes=VMEM_LIMIT)


def _const_spec(shape):
    return pl.BlockSpec(shape, lambda *_: (0,) * len(shape))


def _rms(x, g):
    return (x * lax.rsqrt(jnp.mean(x * x, axis=-1, keepdims=True) + RMS_EPS)) * g


def _dot(a, b):
    return jnp.dot(a, b, preferred_element_type=F32)


def _dot_nt(a, b):
    return lax.dot_general(a, b, (((1,), (1,)), ((), ())), preferred_element_type=F32)


def _dot_tn(a, b):
    return lax.dot_general(a, b, (((0,), (0,)), ((), ())), preferred_element_type=F32)


def _split_dot(a, b):
    hi = a.astype(BF16)
    lo = (a - hi.astype(F32)).astype(BF16)
    return _dot(hi, b) + _dot(lo, b)


def _log_sigmoid(x):
    return jnp.minimum(x, 0.0) - jnp.log1p(jnp.exp(-jnp.abs(x)))


def _gelu_tanh(x):
    c = 0.7978845608028654
    return x * (0.5 * (1.0 + jnp.tanh(c * (x + 0.044715 * (x * x * x)))))


def _inproj_body(x_ref, g_ref, w_ref, wg_ref, qn_ref, kn_ref, bg_ref, cos_ref, sa_ref, sb_ref,
                 ind_ref, indt_ref,
                 qa_ref, ka_ref, va_ref, qb_ref, kb_ref, vb_ref, og_ref, gt_ref, *, wa, transpose_gates):
    h = _rms(x_ref[...], g_ref[...]).astype(BF16)
    reps = wa // LANES
    cos = jnp.concatenate([cos_ref[...]] * reps, axis=1)
    sa = jnp.concatenate([sa_ref[...]] * reps, axis=1)
    sb = jnp.concatenate([sb_ref[...]] * reps, axis=1)
    ind = ind_ref[...]
    indt = indt_ref[...]

    def head_norm_rope(z, gain):
        ss = _split_dot(z * z, ind)
        inv = lax.rsqrt(ss * (1.0 / HEAD_DIM) + RMS_EPS)
        zn = (z * _split_dot(inv, indt)) * gain
        return zn * cos + pltpu.roll(zn, wa - ROT_DIM // 2, 1) * sa + pltpu.roll(zn, ROT_DIM // 2, 1) * sb

    qa_ref[...] = head_norm_rope(_dot(h, w_ref[:, 0:wa]), qn_ref[...]) * (HEAD_DIM ** -0.5)
    ka_ref[...] = head_norm_rope(_dot(h, w_ref[:, wa:2 * wa]), kn_ref[...])
    va_ref[...] = _dot(h, w_ref[:, 2 * wa:3 * wa])
    wb = (w_ref.shape[1] - 3 * wa) // 4
    o = 3 * wa
    qb_ref[...] = _dot(h, w_ref[:, o:o + wb]).astype(qb_ref.dtype)
    kb_ref[...] = (_dot(h, w_ref[:, o + wb:o + 2 * wb]) * (HEAD_DIM ** -0.5)).astype(kb_ref.dtype)
    vb_ref[...] = _dot(h, w_ref[:, o + 2 * wb:o + 3 * wb]).astype(vb_ref.dtype)
    og_ref[...] = jax.nn.sigmoid(_dot(h, w_ref[:, o + 3 * wb:o + 4 * wb]))

    v = _dot(h, wg_ref[...]) + bg_ref[...]
    lane = lax.broadcasted_iota(jnp.int32, v.shape, 1)
    ga = jnp.where(lane >= GATE_LANE0 + 8, _log_sigmoid(v), v)
    if transpose_gates:
        gt_ref[0] = ga.T[GATE_LANE0:, :]
    else:
        gt_ref[...] = ga


def _inproj(x, gain, w_main, w_gate, qn, kn, bg, cos, sa, sb, ind, indt, *, tm, seq_tiles, batch, mdtype,
            transpose_gates):
    t, d = x.shape
    wa = qn.shape[1]
    wb = (w_main.shape[1] - 3 * wa) // 4
    nt = t // tm
    tok = lambda w: pl.BlockSpec((tm, w), lambda i: (i, 0))
    tab = pl.BlockSpec((tm, LANES), lambda i: (i % seq_tiles, 0))
    if transpose_gates:
        seq = seq_tiles * tm
        gt_shape = jax.ShapeDtypeStruct((batch, 16, seq), F32)
        gt_spec = pl.BlockSpec((1, 16, tm), lambda i: (i // seq_tiles, 0, i % seq_tiles))
    else:
        gt_shape = jax.ShapeDtypeStruct((t, LANES), F32)
        gt_spec = tok(LANES)
    out_shape = (
        jax.ShapeDtypeStruct((t, wa), F32), jax.ShapeDtypeStruct((t, wa), F32), jax.ShapeDtypeStruct((t, wa), F32),
        jax.ShapeDtypeStruct((t, wb), mdtype), jax.ShapeDtypeStruct((t, wb), mdtype),
        jax.ShapeDtypeStruct((t, wb), mdtype), jax.ShapeDtypeStruct((t, wb), F32), gt_shape)
    return pl.pallas_call(
        functools.partial(_inproj_body, wa=wa, transpose_gates=transpose_gates),
        grid=(nt,),
        in_specs=[tok(d), _const_spec(gain.shape), _const_spec(w_main.shape), _const_spec(w_gate.shape),
                  _const_spec(qn.shape), _const_spec(kn.shape), _const_spec(bg.shape), tab, tab, tab,
                  _const_spec(ind.shape), _const_spec(indt.shape)],
        out_specs=(tok(wa), tok(wa), tok(wa), tok(wb), tok(wb), tok(wb), tok(wb), gt_spec),
        out_shape=out_shape,
        compiler_params=_cparams(1),
        name="inproj",
    )(x, gain, w_main, w_gate, qn, kn, bg, cos, sa, sb, ind, indt)


def _attn_body(q_ref, k_ref, v_ref, o_ref, acc_ref, m0_ref, m1_ref, l0_ref, l1_ref, *, seq):
    qi = lax.broadcasted_iota(jnp.int32, (2 * BAND, 2 * BAND), 0) % BAND
    kj = lax.broadcasted_iota(jnp.int32, (2 * BAND, 2 * BAND), 1)
    band = (kj >= qi) & (kj <= qi + BAND)
    in_cur = kj >= BAND
    tri = (lax.broadcasted_iota(jnp.int32, (2 * BAND, BAND), 1)
           <= lax.broadcasted_iota(jnp.int32, (2 * BAND, BAND), 0) % BAND)
    head0 = lax.broadcasted_iota(jnp.int32, (BAND, LANES), 1) < HEAD_DIM

    for pi, dil in enumerate(DILATIONS):
        nb = seq // (dil * BAND)
        first, last = pi == 0, pi == len(DILATIONS) - 1
        grp = min(ATTN_GROUP, nb)
        nres = ATTN_GROUP // grp
        ngrp = nb // grp
        step = BAND * dil

        def group(idx, carry, dil=dil, first=first, last=last, grp=grp, nres=nres, ngrp=ngrp, step=step):
            rr, gi = (idx // ngrp, idx % ngrp) if ngrp > 1 else (idx, 0)
            tasks = []
            for ri in range(nres):
                r = rr * nres + ri
                base = r + gi * (grp * step)
                rows = [pl.ds(base + j * step, BAND, stride=dil) for j in range(grp)]
                kv_rows = list(rows)
                if ngrp > 1:
                    kv_rows.insert(0, pl.ds(jnp.maximum(base - step, r), BAND, stride=dil))
                kb = [k_ref[0, x, :].astype(BF16) for x in kv_rows]
                vb = [v_ref[0, x, :].astype(BF16) for x in kv_rows]
                for j in range(grp):
                    jj = j + (1 if ngrp > 1 else 0)
                    q = q_ref[0, rows[j], :]
                    old = None
                    if not first:
                        old = (jnp.concatenate([m0_ref[rows[j], :], m1_ref[rows[j], :]], axis=0),
                               jnp.concatenate([l0_ref[rows[j], :], l1_ref[rows[j], :]], axis=0),
                               acc_ref[rows[j], :])
                    if jj == 0:
                        kk, vv, mask = kb[0], vb[0], tri
                    else:
                        kk = jnp.concatenate([kb[jj - 1], kb[jj]], axis=0)
                        vv = jnp.concatenate([vb[jj - 1], vb[jj]], axis=0)
                        mask = band & (in_cur | (gi > 0)) if (j == 0 and ngrp > 1) else band
                    tasks.append((rows[j], q, kk, vv, mask, old))
            results = []
            for rows_j, q, kk, vv, mask, old in tasks:
                qq = jnp.concatenate([jnp.where(head0, q, 0.0), jnp.where(head0, 0.0, q)], axis=0).astype(BF16)
                s = jnp.where(mask, _dot_nt(qq, kk), NEG_INF)
                rm = jnp.max(s, axis=1, keepdims=True)
                if first:
                    mn = jnp.broadcast_to(rm, (2 * BAND, LANES))
                else:
                    mn = jnp.maximum(old[0], rm)
                p = jnp.exp(s - jnp.concatenate([mn] * (s.shape[1] // LANES), axis=1))
                ls = jnp.sum(p, axis=1, keepdims=True)
                pv = _dot(p.astype(BF16), vv)
                acc = jnp.where(head0, pv[:BAND], pv[BAND:])
                if first:
                    ln = jnp.broadcast_to(ls, (2 * BAND, LANES))
                else:
                    a = jnp.exp(old[0] - mn)
                    ln = a * old[1] + ls
                    acc = jnp.where(head0, a[:BAND], a[BAND:]) * old[2] + acc
                results.append((rows_j, mn, ln, acc))
            for rows_j, mn, ln, acc in results:
                if last:
                    o_ref[0, rows_j, :] = acc / jnp.where(head0, ln[:BAND], ln[BAND:])
                else:
                    m0_ref[rows_j, :] = mn[:BAND]
                    m1_ref[rows_j, :] = mn[BAND:]
                    l0_ref[rows_j, :] = ln[:BAND]
                    l1_ref[rows_j, :] = ln[BAND:]
                    acc_ref[rows_j, :] = acc
            return carry

        lax.fori_loop(0, (dil * nb) // ATTN_GROUP, group, 0)


def _attention(q, k, v):
    b, s, wa = q.shape
    spec = pl.BlockSpec((1, s, LANES), lambda i, j: (i, 0, j))
    scratch = [pltpu.VMEM((s, LANES), F32) for _ in range(5)]
    return pl.pallas_call(
        functools.partial(_attn_body, seq=s),
        grid=(b, wa // LANES),
        in_specs=[spec, spec, spec],
        out_specs=spec,
        out_shape=jax.ShapeDtypeStruct((b, s, wa), F32),
        scratch_shapes=scratch,
        compiler_params=_cparams(2),
        name="dilated_attn",
    )(q, k, v)


def _mlstm_body(q_ref, k_ref, v_ref, g_ref, h_ref, c_ref, n_ref, m_ref, cst_ref, nst_ref, mst_ref, *, nchunks, nheads):
    L = CHUNK
    npairs = nheads // 2
    row = lax.broadcasted_iota(jnp.int32, (L, L), 0)
    lane = lax.broadcasted_iota(jnp.int32, (L, L), 1)
    lane_h = lax.broadcasted_iota(jnp.int32, (nheads, L), 1)
    causal = lane <= row
    diag = lane == row
    head0 = lane < HEAD_DIM
    blockdiag = (row < HEAD_DIM) == head0
    ones = jnp.ones((L, L), BF16)

    @pl.when(pl.program_id(1) == 0)
    def _():
        cst_ref[...] = jnp.zeros_like(cst_ref)
        nst_ref[...] = jnp.zeros_like(nst_ref)
        mst_ref[...] = jnp.zeros_like(mst_ref)

    def to_col(r):
        return jnp.sum(jnp.where(diag, r, 0.0), axis=1, keepdims=True)

    def chunk(c, carry):
        t0 = pl.multiple_of(c * L, L)
        ig_all = g_ref[0, 0:nheads, pl.ds(t0, L)]
        b_all = g_ref[0, nheads:2 * nheads, pl.ds(t0, L)]
        sh = 1
        while sh < L:
            b_all = b_all + jnp.where(lane_h >= sh, pltpu.roll(b_all, sh, 1), 0.0)
            sh *= 2
        g_all = ig_all - b_all
        m_all = mst_ref[...]
        m_out = []
        for p in range(npairs):
            lanes = slice(p * LANES, (p + 1) * LANES)
            q = q_ref[0, pl.ds(t0, L), lanes]
            k = k_ref[0, pl.ds(t0, L), lanes]
            v = v_ref[0, pl.ds(t0, L), lanes]
            zero = jnp.zeros_like(q)
            qq = jnp.concatenate([jnp.where(head0, q, zero), jnp.where(head0, zero, q)], axis=0)
            qk = _dot_nt(qq, k)
            cst = cst_ref[p]
            nst = nst_ref[p]
            qcn = _dot(q, jnp.concatenate([cst.astype(BF16), nst.astype(BF16)], axis=1))

            sc, iw, rs, en, ws, cw = [], [], [], [], [], []
            for h in range(2):
                hh = 2 * p + h
                b_row = b_all[hh:hh + 1, :]
                g_row = g_all[hh:hh + 1, :]
                bc = to_col(b_row)
                m_prev = m_all[hh:hh + 1, 0:1]
                d = jnp.where(causal, bc + g_row, NEG_INF)
                inter = bc + m_prev
                m_t = jnp.maximum(inter, jnp.max(d, axis=1, keepdims=True))
                dw = jnp.exp(d - m_t)
                s_h = qk[h * L:(h + 1) * L, :] * dw
                sc.append(s_h)
                iw.append(jnp.exp(inter - m_t))
                rs.append(jnp.sum(s_h, axis=1, keepdims=True))
                en.append(jnp.exp(-m_t))
                m_new = m_t[L - 1:L, :]
                b_last = bc[L - 1:L, :]
                ws.append(to_col(jnp.exp((b_last + g_row) - m_new)))
                cw.append(jnp.exp((b_last + m_prev) - m_new))
                m_out.append(jnp.broadcast_to(m_new, (1, L)))

            scv = _dot(jnp.concatenate(sc, axis=0).astype(BF16), v)
            intra = jnp.where(head0, scv[:L], scv[L:])
            iw_l = jnp.where(head0, iw[0], iw[1])
            num = iw_l * qcn[:, :L] + intra
            den = iw_l * qcn[:, L:] + jnp.where(head0, rs[0], rs[1])
            h_ref[0, pl.ds(t0, L), lanes] = num / jnp.maximum(jnp.abs(den), jnp.where(head0, en[0], en[1]))

            kw = (k.astype(F32) * jnp.where(head0, ws[0], ws[1])).astype(BF16)
            u = _dot_tn(kw, jnp.concatenate([v, ones], axis=1))
            cw_l = jnp.where(head0, cw[0], cw[1])
            cst_ref[p] = cw_l * cst + jnp.where(blockdiag, u[:, :L], 0.0)
            nst_ref[p] = cw_l * nst + jnp.where(blockdiag, u[:, L:], 0.0)
        mst_ref[...] = jnp.concatenate(m_out, axis=0)
        return carry

    lax.fori_loop(0, nchunks, chunk, 0)

    @pl.when(pl.program_id(1) == pl.num_programs(1) - 1)
    def _():
        for p in range(npairs):
            ct = cst_ref[p].T
            nt = nst_ref[p].T
            for h in range(2):
                sl = slice(h * HEAD_DIM, (h + 1) * HEAD_DIM)
                c_ref[0, 2 * p + h] = ct[sl, sl]
                n_ref[0, 2 * p + h:2 * p + h + 1, :] = nt[h * HEAD_DIM:h * HEAD_DIM + 1, sl]
        m_ref[0] = mst_ref[...]


def _mlstm(q, k, v, gates, *, ts):
    b, s, wb = q.shape
    nheads = wb // HEAD_DIM
    spec = pl.BlockSpec((1, ts, wb), lambda i, j: (i, j, 0))
    out_shape = (jax.ShapeDtypeStruct((b, s, wb), F32),
                 jax.ShapeDtypeStruct((b, nheads, HEAD_DIM, HEAD_DIM), F32),
                 jax.ShapeDtypeStruct((b, nheads, HEAD_DIM), F32),
                 jax.ShapeDtypeStruct((b, nheads, LANES), F32))
    out_specs = (spec,
                 pl.BlockSpec((1, nheads, HEAD_DIM, HEAD_DIM), lambda i, j: (i, 0, 0, 0)),
                 pl.BlockSpec((1, nheads, HEAD_DIM), lambda i, j: (i, 0, 0)),
                 pl.BlockSpec((1, nheads, LANES), lambda i, j: (i, 0, 0)))
    return pl.pallas_call(
        functools.partial(_mlstm_body, nchunks=ts // CHUNK, nheads=nheads),
        grid=(b, s // ts),
        in_specs=[spec, spec, spec, pl.BlockSpec((1, 2 * nheads, ts), lambda i, j: (i, 0, j))],
        out_specs=out_specs,
        out_shape=out_shape,
        scratch_shapes=[pltpu.VMEM((nheads // 2, LANES, LANES), F32), pltpu.VMEM((nheads // 2, LANES, LANES), F32),
                        pltpu.VMEM((nheads, LANES), F32)],
        compiler_params=_cparams(2),
        name="mlstm",
    )(q, k, v, gates)


def _merge_body(x_ref, oa_ref, hb_ref, og_ref, ga_ref, gb_ref, w_ref, o_ref):
    wa = oa_ref.shape[1]
    ya = _rms(oa_ref[...], ga_ref[...]).astype(BF16)
    yb = (og_ref[...] * _rms(hb_ref[...], gb_ref[...])).astype(BF16)
    o_ref[...] = x_ref[...] + _dot(ya, w_ref[0:wa, :]) + _dot(yb, w_ref[wa:, :])


def _merge(x, oa, hb, og, ga, gb, w, *, tm):
    t, d = x.shape
    tok = lambda w_: pl.BlockSpec((tm, w_), lambda i: (i, 0))
    return pl.pallas_call(
        _merge_body,
        grid=(t // tm,),
        in_specs=[tok(d), tok(oa.shape[1]), tok(hb.shape[1]), tok(og.shape[1]),
                  _const_spec(ga.shape), _const_spec(gb.shape), _const_spec(w.shape)],
        out_specs=tok(d),
        out_shape=jax.ShapeDtypeStruct((t, d), F32),
        compiler_params=_cparams(1),
        name="merge_out",
    )(x, oa, hb, og, ga, gb, w)


def _ffn_body(*refs, prompt, seq_tiles, ck):
    if prompt:
        x_ref, g_ref, wup_ref, cw_ref, cb_ref, wdn_ref, o_ref, conv_ref, y_ref, carry_ref = refs
    else:
        x_ref, g_ref, wup_ref, cw_ref, cb_ref, wdn_ref, p0_ref, p1_ref, o_ref, conv_ref, y_ref = refs
    tm = x_ref.shape[0]
    dff = wdn_ref.shape[0]
    x = x_ref[...]
    h = _rms(x, g_ref[...]).astype(BF16)
    if prompt:
        @pl.when(pl.program_id(0) % seq_tiles == 0)
        def _():
            carry_ref[...] = jnp.zeros_like(carry_ref)
        row = lax.broadcasted_iota(jnp.int32, (tm, ck), 0)
    for c in range(dff // ck):
        cs = slice(c * ck, (c + 1) * ck)
        gate = _dot(h, wup_ref[:, c * ck:(c + 1) * ck])
        val = _dot(h, wup_ref[:, dff + c * ck:dff + (c + 1) * ck])
        if prompt:
            b0 = carry_ref[0:1, cs]
            b1 = carry_ref[1:2, cs]
            g1 = jnp.where(row == 0, b1, pltpu.roll(gate, 1, 0))
            g2 = jnp.where(row == 0, b0, jnp.where(row == 1, b1, pltpu.roll(gate, 2, 0)))
            carry_ref[:, cs] = gate[tm - 2:tm, :]
        else:
            g2 = p0_ref[:, cs]
            g1 = p1_ref[:, cs]
            conv_ref[:, cs] = gate
        gc = cb_ref[:, cs] + (cw_ref[0:1, cs] * g2 + cw_ref[1:2, cs] * g1 + cw_ref[2:3, cs] * gate)
        y_ref[:, cs] = (_gelu_tanh(gc) * val).astype(BF16)
    if prompt:
        conv_ref[0] = carry_ref[...]
    o_ref[...] = x + _dot(y_ref[...], wdn_ref[...])


def _ffn(x, gain, w_up, conv_w, conv_b, w_down, prev=None, *, tm, seq_tiles=1, batch=1):
    t, d = x.shape
    dff = w_down.shape[0]
    prompt = prev is None
    tok = lambda w_: pl.BlockSpec((tm, w_), lambda i: (i, 0))
    in_specs = [tok(d), _const_spec(gain.shape), _const_spec(w_up.shape), _const_spec(conv_w.shape),
                _const_spec(conv_b.shape), _const_spec(w_down.shape)]
    args = [x, gain, w_up, conv_w, conv_b, w_down]
    scratch = [pltpu.VMEM((tm, dff), BF16)]
    if prompt:
        conv_shape = jax.ShapeDtypeStruct((batch, 2, dff), F32)
        conv_spec = pl.BlockSpec((1, 2, dff), lambda i: (i // seq_tiles, 0, 0))
        scratch.append(pltpu.VMEM((2, dff), F32))
    else:
        in_specs += [tok(dff), tok(dff)]
        args += list(prev)
        conv_shape = jax.ShapeDtypeStruct((t, dff), F32)
        conv_spec = tok(dff)
    return pl.pallas_call(
        functools.partial(_ffn_body, prompt=prompt, seq_tiles=seq_tiles, ck=256),
        grid=(t // tm,),
        in_specs=in_specs,
        out_specs=(tok(d), conv_spec),
        out_shape=(jax.ShapeDtypeStruct((t, d), F32), conv_shape),
        scratch_shapes=scratch,
        compiler_params=_cparams(1),
        name="conv_glu",
    )(*args)


def _ple_body(x_ref, p_ref, g_ref, wg_ref, wp_ref, o_ref):
    x = x_ref[...]
    gate = jax.nn.sigmoid(_dot(_rms(x, g_ref[...]).astype(BF16), wg_ref[...]))
    o_ref[...] = x + gate * _dot(p_ref[...].astype(BF16), wp_ref[...])


def _ple(x, p, gain, w_gate, w_proj, *, tm):
    t, d = x.shape
    tok = lambda w_: pl.BlockSpec((tm, w_), lambda i: (i, 0))
    return pl.pallas_call(
        _ple_body,
        grid=(t // tm,),
        in_specs=[tok(d), tok(p.shape[1]), _const_spec(gain.shape), _const_spec(w_gate.shape),
                  _const_spec(w_proj.shape)],
        out_specs=tok(d),
        out_shape=jax.ShapeDtypeStruct((t, d), F32),
        compiler_params=_cparams(1),
        name="ple",
    )(x, p, gain, w_gate, w_proj)


def _sattn_body(q_ref, kn_ref, vn_ref, kt_ref, vt_ref, o_ref):
    q = q_ref[0]
    kn = kn_ref[0]
    vn = vn_ref[0]
    nh, _, wbuf = kt_ref.shape
    r = lax.broadcasted_iota(jnp.int32, (HEAD_DIM, HEAD_DIM), 0)
    c = lax.broadcasted_iota(jnp.int32, (HEAD_DIM, HEAD_DIM), 1)
    diag = r == c
    s_new = jnp.sum(q * kn, axis=-1, keepdims=True)
    rows = []
    for h in range(nh):
        q_col = jnp.sum(jnp.where(diag, q[h:h + 1, :], 0.0), axis=1, keepdims=True)
        rows.append(jnp.sum(kt_ref[h] * q_col, axis=0, keepdims=True))
    s = jnp.concatenate(rows, axis=0)
    dist = wbuf - lax.broadcasted_iota(jnp.int32, (1, wbuf), 1)
    stats = []
    for dil in DILATIONS:
        mask = ((dist & (dil - 1)) == 0) & (dist <= BAND * dil)
        sg = jnp.where(mask, s, NEG_INF)
        m = jnp.maximum(jnp.max(sg, axis=1, keepdims=True), s_new)
        e = jnp.exp(sg - m)
        e_new = jnp.exp(s_new - m)
        stats.append((m, e, e_new, jnp.sum(e, axis=1, keepdims=True) + e_new))
    mm = jnp.maximum(jnp.maximum(stats[0][0], stats[1][0]), stats[2][0])
    cs = [jnp.exp(m - mm) for m, _, _, _ in stats]
    wgt = cs[0] * stats[0][1] + cs[1] * stats[1][1] + cs[2] * stats[2][1]
    w_new = cs[0] * stats[0][2] + cs[1] * stats[1][2] + cs[2] * stats[2][2]
    den = cs[0] * stats[0][3] + cs[1] * stats[1][3] + cs[2] * stats[2][3]
    outs = []
    for h in range(nh):
        o_col = jnp.sum(vt_ref[h] * wgt[h:h + 1, :], axis=1, keepdims=True)
        outs.append(jnp.sum(jnp.where(diag, o_col, 0.0), axis=0, keepdims=True))
    o_ref[0] = (jnp.concatenate(outs, axis=0) + w_new * vn) / den


def _sample_attention(q, kn, vn, ckt, cvt, layer):
    b, nh, dh = q.shape
    wbuf = ckt.shape[-1]
    row = pl.BlockSpec((1, nh, dh), lambda i: (i, 0, 0))
    cache = pl.BlockSpec((None, None, nh, dh, wbuf), lambda i: (layer, i, 0, 0, 0))
    return pl.pallas_call(
        _sattn_body,
        grid=(b,),
        in_specs=[row, row, row, cache, cache],
        out_specs=row,
        out_shape=jax.ShapeDtypeStruct((b, nh, dh), F32),
        compiler_params=_cparams(1),
        name="sample_attn",
    )(q, kn, vn, ckt, cvt)


def _smlstm_body(q_ref, k_ref, v_ref, ig_ref, lf_ref, c_ref, n_ref, m_ref, h_ref, co_ref, no_ref, mo_ref):
    q, k, v = q_ref[0], k_ref[0], v_ref[0]
    ig, lf, m = ig_ref[0], lf_ref[0], m_ref[0]
    n = n_ref[0]
    nh = q.shape[0]
    inter = lf + m
    m_t = jnp.maximum(inter, ig)
    dw = jnp.exp(ig - m_t)
    iw = jnp.exp(inter - m_t)
    sc = jnp.sum(q * k, axis=-1, keepdims=True) * dw
    r = lax.broadcasted_iota(jnp.int32, (HEAD_DIM, HEAD_DIM), 0)
    l = lax.broadcasted_iota(jnp.int32, (HEAD_DIM, HEAD_DIM), 1)
    cq = []
    for h in range(nh):
        c_h = c_ref[0, h]
        q_h = jnp.broadcast_to(q[h:h + 1, :], (8, HEAD_DIM)).astype(BF16)
        cq.append(_dot_nt(q_h, c_h.astype(BF16))[0:1, :])
        v_col = jnp.sum(jnp.where(r == l, v[h:h + 1, :], 0.0), axis=1, keepdims=True)
        co_ref[0, h] = iw[h:h + 1, :] * c_h + (dw[h:h + 1, :] * v_col) * k[h:h + 1, :]
    num = iw * jnp.concatenate(cq, axis=0) + sc * v
    den = iw * jnp.sum(n * q, axis=-1, keepdims=True) + sc
    h_ref[0] = num / jnp.maximum(jnp.abs(den), jnp.exp(-m_t))
    no_ref[0] = iw * n + dw * k
    mo_ref[0] = m_t


def _sample_mlstm(q, k, v, ig, lf, c, n, m):
    b, nh, dh = q.shape
    row = pl.BlockSpec((1, nh, dh), lambda i: (i, 0, 0))
    col = pl.BlockSpec((1, nh, 1), lambda i: (i, 0, 0))
    cst = pl.BlockSpec((1, nh, dh, dh), lambda i: (i, 0, 0, 0))
    return pl.pallas_call(
        _smlstm_body,
        grid=(b,),
        in_specs=[row, row, row, col, col, cst, row, col],
        out_specs=(row, cst, row, col),
        out_shape=(jax.ShapeDtypeStruct((b, nh, dh), F32), jax.ShapeDtypeStruct((b, nh, dh, dh), F32),
                   jax.ShapeDtypeStruct((b, nh, dh), F32), jax.ShapeDtypeStruct((b, nh, 1), F32)),
        compiler_params=_cparams(1),
        name="sample_mlstm",
    )(q, k, v, ig, lf, c, n, m)


def _rope_tables(pos):
    half = ROT_DIM // 2
    inv = ROPE_THETA ** (-jnp.arange(half, dtype=F32) / half)
    ang = pos.astype(F32)[:, None] * inv[None, :]
    cos, sin = jnp.cos(ang), jnp.sin(ang)
    n = pos.shape[0]
    rest = HEAD_DIM - ROT_DIM
    c = jnp.concatenate([cos, cos, jnp.ones((n, rest), F32)], axis=1)
    sa = jnp.concatenate([-sin, jnp.zeros((n, half + rest), F32)], axis=1)
    sb = jnp.concatenate([jnp.zeros((n, half), F32), sin, jnp.zeros((n, rest), F32)], axis=1)
    rep = LANES // HEAD_DIM
    return tuple(jnp.concatenate([t] * rep, axis=1) for t in (c, sa, sb))


def kernel(x_prompt, x_sample, cache_win_k, cache_win_v, state_C, state_n, state_m, state_conv, p_prompt, p_sample,
           norm_mix, w_in, q_norm, k_norm, b_gates, out_norm_a, out_norm_b, w_out, norm_ffn, w_up, conv_w, conv_b,
           w_down, norm_ple, w_ple_gate, w_ple_proj):
    batch, seq, d = x_prompt.shape
    dec_batch, dec_seq, _ = x_sample.shape
    depth = w_in.shape[0]
    nh_a = cache_win_k.shape[3]
    nh_b = state_C.shape[2]
    wa, wb = nh_a * HEAD_DIM, nh_b * HEAD_DIM
    wbuf = cache_win_k.shape[2]
    win_max = BAND * DILATIONS[-1]
    keep = min(win_max, seq)
    assert dec_seq == 1 and wbuf == win_max and seq % win_max == 0
    assert w_in.shape[2] == 3 * wa + 4 * wb + 2 * nh_b and wa % LANES == 0 and wb % LANES == 0

    tm = 512
    seq_tiles = seq // tm
    tp, ts = batch * seq, dec_batch * dec_seq

    cut = 3 * wa + 4 * wb
    w_main = w_in[:, :, :cut].astype(BF16)
    w_gate = w_in[:, :, cut + 2 * nh_b - LANES:].astype(BF16)
    bg = jnp.concatenate([jnp.zeros((depth, GATE_LANE0), F32), b_gates], axis=1)[:, None, :]
    qn = jnp.tile(q_norm, (1, nh_a))[:, None, :]
    kn = jnp.tile(k_norm, (1, nh_a))[:, None, :]
    w_out_b, w_up_b, w_down_b = w_out.astype(BF16), w_up.astype(BF16), w_down.astype(BF16)
    w_pg_b, w_pp_b = w_ple_gate.astype(BF16), w_ple_proj.astype(BF16)
    head_of_lane = jnp.arange(wa) // HEAD_DIM
    ind = (head_of_lane[:, None] == jnp.arange(LANES)[None, :]).astype(BF16)
    indt = ind.T
    tab_p = _rope_tables(jnp.arange(seq, dtype=jnp.int32))
    tab_s = tuple(jnp.broadcast_to(t, (ts, LANES))
                  for t in _rope_tables(PAST_LEN + jnp.arange(dec_seq, dtype=jnp.int32)))

    ckt = jnp.transpose(cache_win_k, (0, 1, 3, 4, 2))
    cvt = jnp.transpose(cache_win_v, (0, 1, 3, 4, 2))

    xp = x_prompt.reshape(tp, d)
    xs = x_sample.reshape(ts, d)
    outs = [[] for _ in range(12)]
    for i in range(depth):
        row = lambda a: a[i][None, :]
        qa, ka, va, qb, kb, vb, og, gt = _inproj(
            xp, row(norm_mix), w_main[i], w_gate[i], qn[i], kn[i], bg[i], *tab_p, ind, indt,
            tm=tm, seq_tiles=seq_tiles, batch=batch, mdtype=BF16, transpose_gates=True)
        r3 = lambda a: a.reshape(batch, seq, a.shape[-1])
        oa = _attention(r3(qa), r3(ka), r3(va))
        hb, c_new, n_new, m_new = _mlstm(r3(qb), r3(kb), r3(vb), gt, ts=min(seq, 1024))
        xp = _merge(xp, oa.reshape(tp, wa), hb.reshape(tp, wb), og, row(out_norm_a), row(out_norm_b), w_out_b[i], tm=tm)
        xp, conv_new = _ffn(xp, row(norm_ffn), w_up_b[i], conv_w[i], row(conv_b), w_down_b[i],
                            tm=tm, seq_tiles=seq_tiles, batch=batch)
        xp = _ple(xp, p_prompt[i].reshape(tp, -1), row(norm_ple), w_pg_b[i], w_pp_b[i], tm=tm)
        outs[0].append(ka.reshape(batch, seq, nh_a, HEAD_DIM)[:, seq - keep:])
        outs[1].append(va.reshape(batch, seq, nh_a, HEAD_DIM)[:, seq - keep:])
        outs[2].append(c_new)
        outs[3].append(n_new)
        outs[4].append(m_new[..., 0])
        outs[5].append(conv_new)
        qa, ka, va, qb, kb, vb, og, ga = _inproj(
            xs, row(norm_mix), w_main[i], w_gate[i], qn[i], kn[i], bg[i], *tab_s, ind, indt,
            tm=ts, seq_tiles=1, batch=dec_batch, mdtype=F32, transpose_gates=False)
        hd = lambda a: a.reshape(dec_batch, -1, HEAD_DIM)
        oa = _sample_attention(hd(qa), hd(ka), hd(va), ckt, cvt, i)
        ig = ga[:, GATE_LANE0:GATE_LANE0 + nh_b, None]
        lf = ga[:, GATE_LANE0 + nh_b:, None]
        hb, c_new, n_new, m_new = _sample_mlstm(hd(qb), hd(kb), hd(vb), ig, lf, state_C[i], state_n[i],
                                                state_m[i][..., None])
        xs = _merge(xs, oa.reshape(ts, wa), hb.reshape(ts, wb), og, row(out_norm_a), row(out_norm_b), w_out_b[i], tm=ts)
        xs, gate_new = _ffn(xs, row(norm_ffn), w_up_b[i], conv_w[i], row(conv_b), w_down_b[i],
                            prev=(state_conv[i][:, 0], state_conv[i][:, 1]), tm=ts)
        xs = _ple(xs, p_sample[i].reshape(ts, -1), row(norm_ple), w_pg_b[i], w_pp_b[i], tm=ts)
        outs[6].append(ka.reshape(dec_batch, dec_seq, nh_a, HEAD_DIM))
        outs[7].append(va.reshape(dec_batch, dec_seq, nh_a, HEAD_DIM))
        outs[8].append(c_new)
        outs[9].append(n_new)
        outs[10].append(m_new[..., 0])
        outs[11].append(jnp.stack([state_conv[i][:, 1], gate_new], axis=1))
    st = jnp.stack
    return (xp.reshape(batch, seq, d), xs.reshape(dec_batch, dec_seq, d)) + tuple(st(o) for o in outs)
```

```python
import functools

import jax
import jax.numpy as jnp
from jax import lax
from jax.experimental import pallas as pl
from jax.experimental.pallas import tpu as pltpu

F32 = jnp.float32
BF16 = jnp.bfloat16

HEAD_DIM = 64
ROT_DIM = HEAD_DIM // 4
ROPE_THETA = 500000.0
RMS_EPS = 1e-6
NEG_INF = -1e30
PAST_LEN = 8192
DILATIONS = (1, 4, 16)
BAND = 128
ATTN_GROUP = 8
CHUNK = 128
FFN_CHUNK = 256
TOKEN_TILE = 512
MLSTM_TILE = 1024
LANES = 128
GATE_LANE0 = LANES - 16
VMEM_LIMIT = 56 * 1024 * 1024


def _cparams(n_axes):
    return pltpu.CompilerParams(dimension_semantics=("arbitrary",) * n_axes,
                                vmem_limit_bytes=VMEM_LIMIT)


def _const_spec(shape):
    return pl.BlockSpec(shape, lambda *_: (0,) * len(shape))


def _layer_spec(stack, layer):
    nd = stack.ndim
    return pl.BlockSpec((None,) + stack.shape[1:], lambda *_: (layer,) + (0,) * (nd - 1),
                        pipeline_mode=pl.Buffered(1))


def _rms(x, g):
    return (x * lax.rsqrt(jnp.mean(x * x, axis=-1, keepdims=True) + RMS_EPS)) * g


def _dot(a, b):
    return jnp.dot(a, b, preferred_element_type=F32)


def _dot_nt(a, b):
    return lax.dot_general(a, b, (((1,), (1,)), ((), ())), preferred_element_type=F32)


def _dot_tn(a, b):
    return lax.dot_general(a, b, (((0,), (0,)), ((), ())), preferred_element_type=F32)


def _split_dot(a, b):
    hi = a.astype(BF16)
    lo = (a - hi.astype(F32)).astype(BF16)
    return _dot(hi, b) + _dot(lo, b)


def _log_sigmoid(x):
    return jnp.minimum(x, 0.0) - jnp.log1p(jnp.exp(-jnp.abs(x)))


def _gelu_tanh(x):
    c = 0.7978845608028654
    return x * (0.5 * (1.0 + jnp.tanh(c * (x + 0.044715 * (x * x * x)))))


def _inproj_body(x_ref, g_ref, w_ref, wg_ref, qn_ref, kn_ref, bg_ref, cos_ref, sa_ref, sb_ref,
                 ind_ref, indt_ref,
                 qa_ref, ka_ref, va_ref, qb_ref, kb_ref, vb_ref, og_ref, gt_ref, *, wa, transpose_gates):
    h = _rms(x_ref[...], g_ref[...]).astype(BF16)
    reps = wa // LANES
    cos = jnp.concatenate([cos_ref[...]] * reps, axis=1)
    sa = jnp.concatenate([sa_ref[...]] * reps, axis=1)
    sb = jnp.concatenate([sb_ref[...]] * reps, axis=1)
    ind = ind_ref[...]
    indt = indt_ref[...]

    def head_norm_rope(z, gain):
        ss = _split_dot(z * z, ind)
        inv = lax.rsqrt(ss * (1.0 / HEAD_DIM) + RMS_EPS)
        zn = (z * _split_dot(inv, indt)) * gain
        return zn * cos + pltpu.roll(zn, wa - ROT_DIM // 2, 1) * sa + pltpu.roll(zn, ROT_DIM // 2, 1) * sb

    qa_ref[...] = head_norm_rope(_dot(h, w_ref[:, 0:wa]), qn_ref[...]) * (HEAD_DIM ** -0.5)
    ka_ref[...] = head_norm_rope(_dot(h, w_ref[:, wa:2 * wa]), kn_ref[...])
    va_ref[...] = _dot(h, w_ref[:, 2 * wa:3 * wa])
    wb = (w_ref.shape[1] - 3 * wa) // 4
    o = 3 * wa
    qb_ref[...] = _dot(h, w_ref[:, o:o + wb]).astype(qb_ref.dtype)
    kb_ref[...] = (_dot(h, w_ref[:, o + wb:o + 2 * wb]) * (HEAD_DIM ** -0.5)).astype(kb_ref.dtype)
    vb_ref[...] = _dot(h, w_ref[:, o + 2 * wb:o + 3 * wb]).astype(vb_ref.dtype)
    og_ref[...] = jax.nn.sigmoid(_dot(h, w_ref[:, o + 3 * wb:o + 4 * wb]))

    v = _dot(h, wg_ref[...]) + bg_ref[...]
    lane = lax.broadcasted_iota(jnp.int32, v.shape, 1)
    ga = jnp.where(lane >= GATE_LANE0 + 8, _log_sigmoid(v), v)
    if transpose_gates:
        gt_ref[0] = ga.T[GATE_LANE0:, :]
    else:
        gt_ref[...] = ga


def _inproj(x, layer, gain, w_main, w_gate, qn, kn, bg, cos, sa, sb, ind, indt, *, tm, seq_tiles, batch, mdtype,
            transpose_gates):
    t, d = x.shape
    wa = qn.shape[2]
    wb = (w_main.shape[2] - 3 * wa) // 4
    nt = t // tm
    tok = lambda w: pl.BlockSpec((tm, w), lambda i: (i, 0))
    tab = pl.BlockSpec((tm, LANES), lambda i: (i % seq_tiles, 0))
    if transpose_gates:
        seq = seq_tiles * tm
        gt_shape = jax.ShapeDtypeStruct((batch, 16, seq), F32)
        gt_spec = pl.BlockSpec((1, 16, tm), lambda i: (i // seq_tiles, 0, i % seq_tiles))
    else:
        gt_shape = jax.ShapeDtypeStruct((t, LANES), F32)
        gt_spec = tok(LANES)
    out_shape = (
        jax.ShapeDtypeStruct((t, wa), F32), jax.ShapeDtypeStruct((t, wa), F32), jax.ShapeDtypeStruct((t, wa), F32),
        jax.ShapeDtypeStruct((t, wb), mdtype), jax.ShapeDtypeStruct((t, wb), mdtype),
        jax.ShapeDtypeStruct((t, wb), mdtype), jax.ShapeDtypeStruct((t, wb), F32), gt_shape)
    return pl.pallas_call(
        functools.partial(_inproj_body, wa=wa, transpose_gates=transpose_gates),
        grid=(nt,),
        in_specs=[tok(d)] + [_layer_spec(a, layer) for a in (gain, w_main, w_gate, qn, kn, bg)] + [tab, tab, tab]
                 + [_const_spec(ind.shape), _const_spec(indt.shape)],
        out_specs=(tok(wa), tok(wa), tok(wa), tok(wb), tok(wb), tok(wb), tok(wb), gt_spec),
        out_shape=out_shape,
        compiler_params=_cparams(1),
        name="inproj",
    )(x, gain, w_main, w_gate, qn, kn, bg, cos, sa, sb, ind, indt)


def _attn_body(q_ref, k_ref, v_ref, o_ref, acc_ref, m0_ref, m1_ref, l0_ref, l1_ref, *, seq):
    qi = lax.broadcasted_iota(jnp.int32, (2 * BAND, 2 * BAND), 0) % BAND
    kj = lax.broadcasted_iota(jnp.int32, (2 * BAND, 2 * BAND), 1)
    band = (kj >= qi) & (kj <= qi + BAND)
    in_cur = kj >= BAND
    tri = (lax.broadcasted_iota(jnp.int32, (2 * BAND, BAND), 1)
           <= lax.broadcasted_iota(jnp.int32, (2 * BAND, BAND), 0) % BAND)
    head0 = lax.broadcasted_iota(jnp.int32, (BAND, LANES), 1) < HEAD_DIM

    for pi, dil in enumerate(DILATIONS):
        nb = seq // (dil * BAND)
        first, last = pi == 0, pi == len(DILATIONS) - 1
        grp = min(ATTN_GROUP, nb)
        nres = ATTN_GROUP // grp
        ngrp = nb // grp
        step = BAND * dil

        def group(idx, carry, dil=dil, first=first, last=last, grp=grp, nres=nres, ngrp=ngrp, step=step):
            rr, gi = (idx // ngrp, idx % ngrp) if ngrp > 1 else (idx, 0)
            tasks = []
            for ri in range(nres):
                r = rr * nres + ri
                base = r + gi * (grp * step)
                rows = [pl.ds(base + j * step, BAND, stride=dil) for j in range(grp)]
                kv_rows = list(rows)
                if ngrp > 1:
                    kv_rows.insert(0, pl.ds(jnp.maximum(base - step, r), BAND, stride=dil))
                kb = [k_ref[0, x, :].astype(BF16) for x in kv_rows]
                vb = [v_ref[0, x, :].astype(BF16) for x in kv_rows]
                for j in range(grp):
                    jj = j + (1 if ngrp > 1 else 0)
                    q = q_ref[0, rows[j], :]
                    old = None
                    if not first:
                        old = (jnp.concatenate([m0_ref[rows[j], :], m1_ref[rows[j], :]], axis=0),
                               jnp.concatenate([l0_ref[rows[j], :], l1_ref[rows[j], :]], axis=0),
                               acc_ref[rows[j], :])
                    if jj == 0:
                        kk, vv, mask = kb[0], vb[0], tri
                    else:
                        kk = jnp.concatenate([kb[jj - 1], kb[jj]], axis=0)
                        vv = jnp.concatenate([vb[jj - 1], vb[jj]], axis=0)
                        mask = band & (in_cur | (gi > 0)) if (j == 0 and ngrp > 1) else band
                    tasks.append((rows[j], q, kk, vv, mask, old))
            results = []
            for rows_j, q, kk, vv, mask, old in tasks:
                qq = jnp.concatenate([jnp.where(head0, q, 0.0), jnp.where(head0, 0.0, q)], axis=0).astype(BF16)
                s = jnp.where(mask, _dot_nt(qq, kk), NEG_INF)
                rm = jnp.max(s, axis=1, keepdims=True)
                if first:
                    mn = jnp.broadcast_to(rm, (2 * BAND, LANES))
                else:
                    mn = jnp.maximum(old[0], rm)
                p = jnp.exp(s - jnp.concatenate([mn] * (s.shape[1] // LANES), axis=1))
                pv = _dot(p.astype(BF16), jnp.concatenate([vv, jnp.ones_like(vv)], axis=1))
                ls = pv[:, LANES:]
                acc = jnp.where(head0, pv[:BAND, :LANES], pv[BAND:, :LANES])
                if first:
                    ln = ls
                else:
                    a = jnp.exp(old[0] - mn)
                    ln = a * old[1] + ls
                    acc = jnp.where(head0, a[:BAND], a[BAND:]) * old[2] + acc
                results.append((rows_j, mn, ln, acc))
            for rows_j, mn, ln, acc in results:
                if last:
                    o_ref[0, rows_j, :] = acc / jnp.where(head0, ln[:BAND], ln[BAND:])
                else:
                    m0_ref[rows_j, :] = mn[:BAND]
                    m1_ref[rows_j, :] = mn[BAND:]
                    l0_ref[rows_j, :] = ln[:BAND]
                    l1_ref[rows_j, :] = ln[BAND:]
                    acc_ref[rows_j, :] = acc
            return carry

        lax.fori_loop(0, (dil * nb) // ATTN_GROUP, group, 0)


def _attention(q, k, v):
    b, s, wa = q.shape
    spec = pl.BlockSpec((1, s, LANES), lambda i, j: (i, 0, j))
    scratch = [pltpu.VMEM((s, LANES), F32) for _ in range(5)]
    return pl.pallas_call(
        functools.partial(_attn_body, seq=s),
        grid=(b, wa // LANES),
        in_specs=[spec, spec, spec],
        out_specs=spec,
        out_shape=jax.ShapeDtypeStruct((b, s, wa), F32),
        scratch_shapes=scratch,
        compiler_params=_cparams(2),
        name="dilated_attn",
    )(q, k, v)


def _mlstm_body(q_ref, k_ref, v_ref, g_ref, h_ref, c_ref, n_ref, m_ref, cst_ref, nst_ref, mst_ref, gs_ref, *,
                nchunks, nheads):
    L = CHUNK
    npairs = nheads // 2
    row = lax.broadcasted_iota(jnp.int32, (L, L), 0)
    lane = lax.broadcasted_iota(jnp.int32, (L, L), 1)
    lane_h = lax.broadcasted_iota(jnp.int32, (nheads, L), 1)
    causal_t = row <= lane
    head0_lane = lane < HEAD_DIM
    head0_row = row < HEAD_DIM
    blockdiag = head0_row == head0_lane
    lane8 = lax.broadcasted_iota(jnp.int32, (8, L), 1)
    row8 = lax.broadcasted_iota(jnp.int32, (8, L), 0)
    pad = jnp.zeros((L - 2 * nheads, L), F32)

    @pl.when(pl.program_id(1) == 0)
    def _():
        cst_ref[...] = jnp.zeros_like(cst_ref)
        nst_ref[...] = jnp.zeros_like(nst_ref)
        mst_ref[...] = jnp.zeros_like(mst_ref)

    def scan_lanes(x, op, fill):
        sh = 1
        while sh < L:
            x = op(x, jnp.where(lane_h >= sh, pltpu.roll(x, sh, 1), fill))
            sh *= 2
        return x

    m_prev = mst_ref[...]
    for c in range(nchunks):
        sl = slice(c * L, (c + 1) * L)
        b_c = scan_lanes(g_ref[0, nheads:2 * nheads, sl], jnp.add, 0.0)
        g_c = g_ref[0, 0:nheads, sl] - b_c
        cm_c = scan_lanes(g_c, jnp.maximum, NEG_INF)
        mm_c = jnp.maximum(cm_c, m_prev)
        mm_last = jnp.maximum(jnp.broadcast_to(cm_c[:, L - 1:L], (nheads, L)), m_prev)
        gs_ref[0, :, sl] = g_c
        gs_ref[1, :, sl] = mm_c
        gs_ref[2, :, sl] = jnp.exp(m_prev - mm_c)
        gs_ref[3, :, sl] = jnp.exp(-(b_c + mm_c))
        gs_ref[4, :, sl] = jnp.exp(g_c - mm_last)
        m_prev = jnp.broadcast_to(b_c[:, L - 1:L], (nheads, L)) + mm_last
    mst_ref[...] = m_prev

    def chunk(c, carry):
        t0 = pl.multiple_of(c * L, L)
        g_all = gs_ref[0, :, pl.ds(t0, L)]
        mm_all = gs_ref[1, :, pl.ds(t0, L)]
        iw_all = gs_ref[2, :, pl.ds(t0, L)]
        en_all = gs_ref[3, :, pl.ds(t0, L)]
        ws_all = gs_ref[4, :, pl.ds(t0, L)]
        cols = jnp.concatenate([g_all, ws_all, pad], axis=0).T
        for p in range(npairs):
            lanes = slice(p * LANES, (p + 1) * LANES)
            h0, h1 = 2 * p, 2 * p + 1
            q = q_ref[0, pl.ds(t0, L), lanes]
            k = k_ref[0, pl.ds(t0, L), lanes]
            v = v_ref[0, pl.ds(t0, L), lanes]
            zero = jnp.zeros_like(q)
            qq = jnp.concatenate([jnp.where(head0_lane, q, zero), jnp.where(head0_lane, zero, q)], axis=0)
            qk = _dot_nt(k, qq)
            cst = cst_ref[p]
            nst = nst_ref[p]
            sc, rs = [], []
            for h, hh in enumerate((h0, h1)):
                d = jnp.where(causal_t, cols[:, hh:hh + 1] - mm_all[hh:hh + 1, :], NEG_INF)
                s_h = qk[:, h * L:(h + 1) * L] * jnp.exp(d)
                sc.append(s_h.astype(BF16))
                rs.append(jnp.sum(s_h, axis=0, keepdims=True))
            ws_l = jnp.where(head0_lane, cols[:, nheads + h0:nheads + h0 + 1], cols[:, nheads + h1:nheads + h1 + 1])
            kws = k.astype(F32) * ws_l
            u = _dot_tn(v, jnp.concatenate(sc + [kws.astype(BF16)], axis=1))
            intra = jnp.where(head0_row, u[:, :L], u[:, L:2 * L])
            inter = _dot_nt(cst.astype(BF16), q)
            qn = _dot_nt(nst.astype(BF16), q)
            iw_l = jnp.where(head0_row, iw_all[h0:h0 + 1, :], iw_all[h1:h1 + 1, :])
            den0 = iw_all[h0:h0 + 1, :] * qn[0:1, :] + rs[0]
            den1 = iw_all[h1:h1 + 1, :] * qn[1:2, :] + rs[1]
            lim = jnp.maximum(jnp.abs(jnp.where(head0_row, den0, den1)),
                              jnp.where(head0_row, en_all[h0:h0 + 1, :], en_all[h1:h1 + 1, :]))
            h_ref[0, pl.ds(t0, L), lanes] = ((iw_l * inter + intra) / lim).T

            cw0 = iw_all[h0:h0 + 1, L - 1:L]
            cw1 = iw_all[h1:h1 + 1, L - 1:L]
            cst_ref[p] = jnp.where(head0_row, cw0, cw1) * cst + jnp.where(blockdiag, u[:, 2 * L:], 0.0)
            ksum = jnp.sum(kws, axis=0, keepdims=True)
            own = (row8 == 0) == (lane8 < HEAD_DIM)
            nst_ref[p] = jnp.where((row8 < 2) & own, jnp.where(row8 == 0, cw0, cw1) * nst + ksum, 0.0)
        return carry

    lax.fori_loop(0, nchunks, chunk, 0)

    @pl.when(pl.program_id(1) == pl.num_programs(1) - 1)
    def _():
        for p in range(npairs):
            for h in range(2):
                sl = slice(h * HEAD_DIM, (h + 1) * HEAD_DIM)
                c_ref[0, 2 * p + h] = cst_ref[p, sl, sl]
                n_ref[0, 2 * p + h:2 * p + h + 1, :] = nst_ref[p, h:h + 1, sl]
        m_ref[0] = mst_ref[...]


def _mlstm(q, k, v, gates, *, ts):
    b, s, wb = q.shape
    nheads = wb // HEAD_DIM
    spec = pl.BlockSpec((1, ts, wb), lambda i, j: (i, j, 0))
    out_shape = (jax.ShapeDtypeStruct((b, s, wb), F32),
                 jax.ShapeDtypeStruct((b, nheads, HEAD_DIM, HEAD_DIM), F32),
                 jax.ShapeDtypeStruct((b, nheads, HEAD_DIM), F32),
                 jax.ShapeDtypeStruct((b, nheads, LANES), F32))
    out_specs = (spec,
                 pl.BlockSpec((1, nheads, HEAD_DIM, HEAD_DIM), lambda i, j: (i, 0, 0, 0)),
                 pl.BlockSpec((1, nheads, HEAD_DIM), lambda i, j: (i, 0, 0)),
                 pl.BlockSpec((1, nheads, LANES), lambda i, j: (i, 0, 0)))
    return pl.pallas_call(
        functools.partial(_mlstm_body, nchunks=ts // CHUNK, nheads=nheads),
        grid=(b, s // ts),
        in_specs=[spec, spec, spec, pl.BlockSpec((1, 2 * nheads, ts), lambda i, j: (i, 0, j))],
        out_specs=out_specs,
        out_shape=out_shape,
        scratch_shapes=[pltpu.VMEM((nheads // 2, LANES, LANES), F32), pltpu.VMEM((nheads // 2, 8, LANES), F32),
                        pltpu.VMEM((nheads, LANES), F32), pltpu.VMEM((5, nheads, ts), F32)],
        compiler_params=_cparams(2),
        name="mlstm",
    )(q, k, v, gates)


def _post_body(*refs, prompt, seq_tiles, ck):
    (x_ref, oa_ref, hb_ref, og_ref, p_ref, ga_ref, gb_ref, wo_ref, g_ref, wup_ref, cw_ref, cb_ref, wdn_ref,
     gp_ref, wpg_ref, wpp_ref) = refs[:16]
    if prompt:
        o_ref, conv_ref, y_ref, carry_ref = refs[16:]
    else:
        p0_ref, p1_ref, o_ref, conv_ref, y_ref = refs[16:]
    tm = x_ref.shape[0]
    dff = wdn_ref.shape[0]
    wa = oa_ref.shape[1]
    ya = _rms(oa_ref[...], ga_ref[...]).astype(BF16)
    yb = (og_ref[...] * _rms(hb_ref[...], gb_ref[...])).astype(BF16)
    x = x_ref[...] + _dot(ya, wo_ref[0:wa, :]) + _dot(yb, wo_ref[wa:, :])
    h = _rms(x, g_ref[...]).astype(BF16)
    if prompt:
        @pl.when(pl.program_id(0) % seq_tiles == 0)
        def _():
            carry_ref[...] = jnp.zeros_like(carry_ref)
        row = lax.broadcasted_iota(jnp.int32, (tm, ck), 0)
    for c in range(dff // ck):
        cs = slice(c * ck, (c + 1) * ck)
        gate = _dot(h, wup_ref[:, c * ck:(c + 1) * ck])
        val = _dot(h, wup_ref[:, dff + c * ck:dff + (c + 1) * ck])
        if prompt:
            b0 = carry_ref[0:1, cs]
            b1 = carry_ref[1:2, cs]
            g1 = jnp.where(row == 0, b1, pltpu.roll(gate, 1, 0))
            g2 = jnp.where(row == 0, b0, jnp.where(row == 1, b1, pltpu.roll(gate, 2, 0)))
            carry_ref[:, cs] = gate[tm - 2:tm, :]
        else:
            g2 = p0_ref[:, cs]
            g1 = p1_ref[:, cs]
            conv_ref[:, cs] = gate
        gc = cb_ref[:, cs] + (cw_ref[0:1, cs] * g2 + cw_ref[1:2, cs] * g1 + cw_ref[2:3, cs] * gate)
        y_ref[:, cs] = (_gelu_tanh(gc) * val).astype(BF16)
    if prompt:
        conv_ref[0] = carry_ref[...]
    x = x + _dot(y_ref[...], wdn_ref[...])
    gate = jax.nn.sigmoid(_dot(_rms(x, gp_ref[...]).astype(BF16), wpg_ref[...]))
    o_ref[...] = x + gate * _dot(p_ref[...].astype(BF16), wpp_ref[...])


def _post(x, oa, hb, og, p, layer, params, prev=None, *, tm, seq_tiles=1, batch=1):
    t, d = x.shape
    dff = params[7].shape[1]
    prompt = prev is None
    tok = lambda w_: pl.BlockSpec((tm, w_), lambda i: (i, 0))
    in_specs = [tok(d), tok(oa.shape[1]), tok(hb.shape[1]), tok(og.shape[1]),
                pl.BlockSpec((None, tm, p.shape[2]), lambda i: (layer, i, 0))]
    in_specs += [_layer_spec(a, layer) for a in params]
    args = [x, oa, hb, og, p, *params]
    scratch = [pltpu.VMEM((tm, dff), BF16)]
    if prompt:
        conv_shape = jax.ShapeDtypeStruct((batch, 2, dff), F32)
        conv_spec = pl.BlockSpec((1, 2, dff), lambda i: (i // seq_tiles, 0, 0))
        scratch.append(pltpu.VMEM((2, dff), F32))
    else:
        in_specs += [tok(dff), tok(dff)]
        args += list(prev)
        conv_shape = jax.ShapeDtypeStruct((t, dff), F32)
        conv_spec = tok(dff)
    return pl.pallas_call(
        functools.partial(_post_body, prompt=prompt, seq_tiles=seq_tiles, ck=FFN_CHUNK),
        grid=(t // tm,),
        in_specs=in_specs,
        out_specs=(tok(d), conv_spec),
        out_shape=(jax.ShapeDtypeStruct((t, d), F32), conv_shape),
        scratch_shapes=scratch,
        compiler_params=_cparams(1),
        name="post_mixer",
    )(*args)


def _sattn_body(q_ref, kn_ref, vn_ref, kt_ref, vt_ref, o_ref):
    q = q_ref[0]
    kn = kn_ref[0]
    vn = vn_ref[0]
    nh, _, wbuf = kt_ref.shape
    r = lax.broadcasted_iota(jnp.int32, (HEAD_DIM, HEAD_DIM), 0)
    c = lax.broadcasted_iota(jnp.int32, (HEAD_DIM, HEAD_DIM), 1)
    diag = r == c
    s_new = jnp.sum(q * kn, axis=-1, keepdims=True)
    rows = []
    for h in range(nh):
        q_col = jnp.sum(jnp.where(diag, q[h:h + 1, :], 0.0), axis=1, keepdims=True)
        rows.append(jnp.sum(kt_ref[h] * q_col, axis=0, keepdims=True))
    s = jnp.concatenate(rows, axis=0)
    dist = wbuf - lax.broadcasted_iota(jnp.int32, (1, wbuf), 1)
    stats = []
    for dil in DILATIONS:
        mask = ((dist & (dil - 1)) == 0) & (dist <= BAND * dil)
        sg = jnp.where(mask, s, NEG_INF)
        m = jnp.maximum(jnp.max(sg, axis=1, keepdims=True), s_new)
        e = jnp.exp(sg - m)
        e_new = jnp.exp(s_new - m)
        stats.append((m, e, e_new, jnp.sum(e, axis=1, keepdims=True) + e_new))
    mm = jnp.maximum(jnp.maximum(stats[0][0], stats[1][0]), stats[2][0])
    cs = [jnp.exp(m - mm) for m, _, _, _ in stats]
    wgt = cs[0] * stats[0][1] + cs[1] * stats[1][1] + cs[2] * stats[2][1]
    w_new = cs[0] * stats[0][2] + cs[1] * stats[1][2] + cs[2] * stats[2][2]
    den = cs[0] * stats[0][3] + cs[1] * stats[1][3] + cs[2] * stats[2][3]
    outs = []
    for h in range(nh):
        o_col = jnp.sum(vt_ref[h] * wgt[h:h + 1, :], axis=1, keepdims=True)
        outs.append(jnp.sum(jnp.where(diag, o_col, 0.0), axis=0, keepdims=True))
    o_ref[0] = (jnp.concatenate(outs, axis=0) + w_new * vn) / den


def _sample_attention(q, kn, vn, ckt, cvt, layer):
    b, nh, dh = q.shape
    wbuf = ckt.shape[-1]
    row = pl.BlockSpec((1, nh, dh), lambda i: (i, 0, 0))
    cache = pl.BlockSpec((None, None, nh, dh, wbuf), lambda i: (layer, i, 0, 0, 0))
    return pl.pallas_call(
        _sattn_body,
        grid=(b,),
        in_specs=[row, row, row, cache, cache],
        out_specs=row,
        out_shape=jax.ShapeDtypeStruct((b, nh, dh), F32),
        compiler_params=_cparams(1),
        name="sample_attn",
    )(q, kn, vn, ckt, cvt)


def _smlstm_body(q_ref, k_ref, v_ref, ig_ref, lf_ref, c_ref, n_ref, m_ref, h_ref, co_ref, no_ref, mo_ref):
    q, k, v = q_ref[0], k_ref[0], v_ref[0]
    ig, lf, m = ig_ref[0], lf_ref[0], m_ref[0]
    n = n_ref[0]
    nh = q.shape[0]
    inter = lf + m
    m_t = jnp.maximum(inter, ig)
    dw = jnp.exp(ig - m_t)
    iw = jnp.exp(inter - m_t)
    sc = jnp.sum(q * k, axis=-1, keepdims=True) * dw
    r = lax.broadcasted_iota(jnp.int32, (HEAD_DIM, HEAD_DIM), 0)
    l = lax.broadcasted_iota(jnp.int32, (HEAD_DIM, HEAD_DIM), 1)
    cq = []
    for h in range(nh):
        c_h = c_ref[0, h]
        q_h = jnp.broadcast_to(q[h:h + 1, :], (8, HEAD_DIM)).astype(BF16)
        cq.append(_dot_nt(q_h, c_h.astype(BF16))[0:1, :])
        v_col = jnp.sum(jnp.where(r == l, v[h:h + 1, :], 0.0), axis=1, keepdims=True)
        co_ref[0, h] = iw[h:h + 1, :] * c_h + (dw[h:h + 1, :] * v_col) * k[h:h + 1, :]
    num = iw * jnp.concatenate(cq, axis=0) + sc * v
    den = iw * jnp.sum(n * q, axis=-1, keepdims=True) + sc
    h_ref[0] = num / jnp.maximum(jnp.abs(den), jnp.exp(-m_t))
    no_ref[0] = iw * n + dw * k
    mo_ref[0] = m_t


def _sample_mlstm(q, k, v, ig, lf, c, n, m):
    b, nh, dh = q.shape
    row = pl.BlockSpec((1, nh, dh), lambda i: (i, 0, 0))
    col = pl.BlockSpec((1, nh, 1), lambda i: (i, 0, 0))
    cst = pl.BlockSpec((1, nh, dh, dh), lambda i: (i, 0, 0, 0))
    return pl.pallas_call(
        _smlstm_body,
        grid=(b,),
        in_specs=[row, row, row, col, col, cst, row, col],
        out_specs=(row, cst, row, col),
        out_shape=(jax.ShapeDtypeStruct((b, nh, dh), F32), jax.ShapeDtypeStruct((b, nh, dh, dh), F32),
                   jax.ShapeDtypeStruct((b, nh, dh), F32), jax.ShapeDtypeStruct((b, nh, 1), F32)),
        compiler_params=_cparams(1),
        name="sample_mlstm",
    )(q, k, v, ig, lf, c, n, m)


def _rope_tables(pos):
    half = ROT_DIM // 2
    inv = ROPE_THETA ** (-jnp.arange(half, dtype=F32) / half)
    ang = pos.astype(F32)[:, None] * inv[None, :]
    cos, sin = jnp.cos(ang), jnp.sin(ang)
    n = pos.shape[0]
    rest = HEAD_DIM - ROT_DIM
    c = jnp.concatenate([cos, cos, jnp.ones((n, rest), F32)], axis=1)
    sa = jnp.concatenate([-sin, jnp.zeros((n, half + rest), F32)], axis=1)
    sb = jnp.concatenate([jnp.zeros((n, half), F32), sin, jnp.zeros((n, rest), F32)], axis=1)
    rep = LANES // HEAD_DIM
    return tuple(jnp.concatenate([t] * rep, axis=1) for t in (c, sa, sb))


def kernel(x_prompt, x_sample, cache_win_k, cache_win_v, state_C, state_n, state_m, state_conv, p_prompt, p_sample,
           norm_mix, w_in, q_norm, k_norm, b_gates, out_norm_a, out_norm_b, w_out, norm_ffn, w_up, conv_w, conv_b,
           w_down, norm_ple, w_ple_gate, w_ple_proj):
    batch, seq, d = x_prompt.shape
    dec_batch, dec_seq, _ = x_sample.shape
    depth = w_in.shape[0]
    nh_a = cache_win_k.shape[3]
    nh_b = state_C.shape[2]
    wa, wb = nh_a * HEAD_DIM, nh_b * HEAD_DIM
    wbuf = cache_win_k.shape[2]
    win_max = BAND * DILATIONS[-1]
    keep = min(win_max, seq)
    assert dec_seq == 1 and wbuf == win_max and seq % win_max == 0
    assert w_in.shape[2] == 3 * wa + 4 * wb + 2 * nh_b and wa % LANES == 0 and wb % LANES == 0

    tm = min(TOKEN_TILE, seq)
    seq_tiles = seq // tm
    tp, ts = batch * seq, dec_batch * dec_seq

    cut = 3 * wa + 4 * wb
    w_main = w_in[:, :, :cut].astype(BF16)
    w_gate = w_in[:, :, cut + 2 * nh_b - LANES:].astype(BF16)
    bg = jnp.concatenate([jnp.zeros((depth, GATE_LANE0), F32), b_gates], axis=1)[:, None, :]
    qn = jnp.tile(q_norm, (1, nh_a))[:, None, :]
    kn = jnp.tile(k_norm, (1, nh_a))[:, None, :]
    rows = lambda a: a[:, None, :]
    inproj_params = (rows(norm_mix), w_main, w_gate, qn, kn, bg)
    post_params = (rows(out_norm_a), rows(out_norm_b), w_out.astype(BF16), rows(norm_ffn), w_up.astype(BF16),
                   conv_w, rows(conv_b), w_down.astype(BF16), rows(norm_ple), w_ple_gate.astype(BF16),
                   w_ple_proj.astype(BF16))
    pp = p_prompt.reshape(depth, tp, -1)
    ps = p_sample.reshape(depth, ts, -1)
    head_of_lane = jnp.arange(wa) // HEAD_DIM
    ind = (head_of_lane[:, None] == jnp.arange(LANES)[None, :]).astype(BF16)
    indt = ind.T
    tab_p = _rope_tables(jnp.arange(seq, dtype=jnp.int32))
    tab_s = tuple(jnp.broadcast_to(t, (ts, LANES))
                  for t in _rope_tables(PAST_LEN + jnp.arange(dec_seq, dtype=jnp.int32)))

    ckt = jnp.transpose(cache_win_k, (0, 1, 3, 4, 2))
    cvt = jnp.transpose(cache_win_v, (0, 1, 3, 4, 2))

    xp = x_prompt.reshape(tp, d)
    xs = x_sample.reshape(ts, d)
    outs = [[] for _ in range(12)]
    for i in range(depth):
        qa, ka, va, qb, kb, vb, og, gt = _inproj(
            xp, i, *inproj_params, *tab_p, ind, indt,
            tm=tm, seq_tiles=seq_tiles, batch=batch, mdtype=BF16, transpose_gates=True)
        r3 = lambda a: a.reshape(batch, seq, a.shape[-1])
        oa = _attention(r3(qa), r3(ka), r3(va))
        hb, c_new, n_new, m_new = _mlstm(r3(qb), r3(kb), r3(vb), gt, ts=min(seq, MLSTM_TILE))
        xp, conv_new = _post(xp, oa.reshape(tp, wa), hb.reshape(tp, wb), og, pp, i, post_params,
                             tm=tm, seq_tiles=seq_tiles, batch=batch)
        outs[0].append(ka.reshape(batch, seq, nh_a, HEAD_DIM)[:, seq - keep:])
        outs[1].append(va.reshape(batch, seq, nh_a, HEAD_DIM)[:, seq - keep:])
        outs[2].append(c_new)
        outs[3].append(n_new)
        outs[4].append(m_new[..., 0])
        outs[5].append(conv_new)
        qa, ka, va, qb, kb, vb, og, ga = _inproj(
            xs, i, *inproj_params, *tab_s, ind, indt,
            tm=ts, seq_tiles=1, batch=dec_batch, mdtype=F32, transpose_gates=False)
        hd = lambda a: a.reshape(dec_batch, -1, HEAD_DIM)
        oa = _sample_attention(hd(qa), hd(ka), hd(va), ckt, cvt, i)
        ig = ga[:, GATE_LANE0:GATE_LANE0 + nh_b, None]
        lf = ga[:, GATE_LANE0 + nh_b:, None]
        hb, c_new, n_new, m_new = _sample_mlstm(hd(qb), hd(kb), hd(vb), ig, lf, state_C[i], state_n[i],
                                                state_m[i][..., None])
        xs, gate_new = _post(xs, oa.reshape(ts, wa), hb.reshape(ts, wb), og, ps, i, post_params,
                             prev=(state_conv[i][:, 0], state_conv[i][:, 1]), tm=ts)
        outs[6].append(ka.reshape(dec_batch, dec_seq, nh_a, HEAD_DIM))
        outs[7].append(va.reshape(dec_batch, dec_seq, nh_a, HEAD_DIM))
        outs[8].append(c_new)
        outs[9].append(n_new)
        outs[10].append(m_new[..., 0])
        outs[11].append(jnp.stack([state_conv[i][:, 1], gate_new], axis=1))
    st = jnp.stack
    return (xp.reshape(batch, seq, d), xs.reshape(dec_batch, dec_seq, d)) + tuple(st(o) for o in outs)
```

```python
import functools

import jax
import jax.numpy as jnp
from jax import lax
from jax.experimental import pallas as pl
from jax.experimental.pallas import tpu as pltpu

F32 = jnp.float32
BF16 = jnp.bfloat16

HEAD_DIM = 64
ROT_DIM = HEAD_DIM // 4
ROPE_THETA = 500000.0
RMS_EPS = 1e-6
NEG_INF = -1e30
PAST_LEN = 8192
DILATIONS = (1, 4, 16)
NRES = 4
BAND = 128
ATTN_GROUP = 8
CHUNK = 128
FFN_CHUNK = 256
TOKEN_TILE = 512
MLSTM_TILE = 1024
LANES = 128
GATE_LANE0 = LANES - 16
VMEM_LIMIT = 56 * 1024 * 1024


def _cparams(n_axes):
    return pltpu.CompilerParams(dimension_semantics=("arbitrary",) * n_axes,
                                vmem_limit_bytes=VMEM_LIMIT)


def _const_spec(shape):
    return pl.BlockSpec(shape, lambda *_: (0,) * len(shape))


def _layer_spec(stack, layer):
    nd = stack.ndim
    return pl.BlockSpec((None,) + stack.shape[1:], lambda *_: (layer,) + (0,) * (nd - 1),
                        pipeline_mode=pl.Buffered(1))


def _rms(x, g):
    return (x * lax.rsqrt(jnp.mean(x * x, axis=-1, keepdims=True) + RMS_EPS)) * g


def _dot(a, b):
    return jnp.dot(a, b, preferred_element_type=F32)


def _dot_nt(a, b):
    return lax.dot_general(a, b, (((1,), (1,)), ((), ())), preferred_element_type=F32)


def _dot_tn(a, b):
    return lax.dot_general(a, b, (((0,), (0,)), ((), ())), preferred_element_type=F32)


def _split_dot(a, b):
    hi = a.astype(BF16)
    lo = (a - hi.astype(F32)).astype(BF16)
    return _dot(hi, b) + _dot(lo, b)


def _log_sigmoid(x):
    return jnp.minimum(x, 0.0) - jnp.log1p(jnp.exp(-jnp.abs(x)))


def _gelu_tanh(x):
    c = 0.7978845608028654
    return x * (0.5 * (1.0 + jnp.tanh(c * (x + 0.044715 * (x * x * x)))))


def _inproj_body(x_ref, g_ref, w_ref, wg_ref, qn_ref, kn_ref, bg_ref, cos_ref, sa_ref, sb_ref,
                 ind_ref, indt_ref,
                 qa_ref, ka_ref, va_ref, qb_ref, kb_ref, vb_ref, og_ref, gt_ref, *rest, wa, transpose_gates):
    h = _rms(x_ref[...], g_ref[...]).astype(BF16)
    reps = wa // LANES
    cos = jnp.concatenate([cos_ref[...]] * reps, axis=1)
    sa = jnp.concatenate([sa_ref[...]] * reps, axis=1)
    sb = jnp.concatenate([sb_ref[...]] * reps, axis=1)
    ind = ind_ref[...]
    indt = indt_ref[...]

    def head_norm_rope(z, gain):
        ss = _split_dot(z * z, ind)
        inv = lax.rsqrt(ss * (1.0 / HEAD_DIM) + RMS_EPS)
        zn = (z * _split_dot(inv, indt)) * gain
        return zn * cos + pltpu.roll(zn, wa - ROT_DIM // 2, 1) * sa + pltpu.roll(zn, ROT_DIM // 2, 1) * sb

    qa = head_norm_rope(_dot(h, w_ref[:, 0:wa]), qn_ref[...]) * (HEAD_DIM ** -0.5)
    ka = head_norm_rope(_dot(h, w_ref[:, wa:2 * wa]), kn_ref[...])
    va = _dot(h, w_ref[:, 2 * wa:3 * wa])
    ka_ref[...] = ka
    va_ref[...] = va
    if transpose_gates:
        krm_ref, vrm_ref, zs_ref = rest
        rows = zs_ref.shape[1] // NRES
        for val, dst in ((qa, qa_ref), (ka, krm_ref), (va, vrm_ref)):
            for c in range(reps):
                zs_ref[c] = val[:, c * LANES:(c + 1) * LANES]
            for r in range(NRES):
                for c in range(reps):
                    dst[0, r, :, c * LANES:(c + 1) * LANES] = zs_ref[c, pl.ds(r, rows, stride=NRES), :]
    else:
        qa_ref[...] = qa
    wb = (w_ref.shape[1] - 3 * wa) // 4
    o = 3 * wa
    qb_ref[...] = _dot(h, w_ref[:, o:o + wb]).astype(qb_ref.dtype)
    kb_ref[...] = (_dot(h, w_ref[:, o + wb:o + 2 * wb]) * (HEAD_DIM ** -0.5)).astype(kb_ref.dtype)
    vb_ref[...] = _dot(h, w_ref[:, o + 2 * wb:o + 3 * wb]).astype(vb_ref.dtype)
    og_ref[...] = jax.nn.sigmoid(_dot(h, w_ref[:, o + 3 * wb:o + 4 * wb]))

    v = _dot(h, wg_ref[...]) + bg_ref[...]
    lane = lax.broadcasted_iota(jnp.int32, v.shape, 1)
    ga = jnp.where(lane >= GATE_LANE0 + 8, _log_sigmoid(v), v)
    if transpose_gates:
        gt_ref[0] = ga.T[GATE_LANE0:, :]
    else:
        gt_ref[...] = ga


def _inproj(x, layer, gain, w_main, w_gate, qn, kn, bg, cos, sa, sb, ind, indt, *, tm, seq_tiles, batch, mdtype,
            transpose_gates):
    t, d = x.shape
    wa = qn.shape[2]
    wb = (w_main.shape[2] - 3 * wa) // 4
    nt = t // tm
    tok = lambda w: pl.BlockSpec((tm, w), lambda i: (i, 0))
    tab = pl.BlockSpec((tm, LANES), lambda i: (i % seq_tiles, 0))
    if transpose_gates:
        seq = seq_tiles * tm
        gt_shape = jax.ShapeDtypeStruct((batch, 16, seq), F32)
        gt_spec = pl.BlockSpec((1, 16, tm), lambda i: (i // seq_tiles, 0, i % seq_tiles))
    else:
        gt_shape = jax.ShapeDtypeStruct((t, LANES), F32)
        gt_spec = tok(LANES)
    out_shape = [
        jax.ShapeDtypeStruct((t, wa), F32), jax.ShapeDtypeStruct((t, wa), F32), jax.ShapeDtypeStruct((t, wa), F32),
        jax.ShapeDtypeStruct((t, wb), mdtype), jax.ShapeDtypeStruct((t, wb), mdtype),
        jax.ShapeDtypeStruct((t, wb), mdtype), jax.ShapeDtypeStruct((t, wb), F32), gt_shape]
    out_specs = [tok(wa), tok(wa), tok(wa), tok(wb), tok(wb), tok(wb), tok(wb), gt_spec]
    scratch = []
    if transpose_gates:
        rm_shape = jax.ShapeDtypeStruct((batch, NRES, seq // NRES, wa), F32)
        rm_spec = pl.BlockSpec((1, NRES, tm // NRES, wa), lambda i: (i // seq_tiles, 0, i % seq_tiles, 0))
        out_shape = [rm_shape] + out_shape[1:] + [rm_shape, rm_shape]
        out_specs = [rm_spec] + out_specs[1:] + [rm_spec, rm_spec]
        scratch = [pltpu.VMEM((wa // LANES, tm, LANES), F32)]
    return pl.pallas_call(
        functools.partial(_inproj_body, wa=wa, transpose_gates=transpose_gates),
        grid=(nt,),
        in_specs=[tok(d)] + [_layer_spec(a, layer) for a in (gain, w_main, w_gate, qn, kn, bg)] + [tab, tab, tab]
                 + [_const_spec(ind.shape), _const_spec(indt.shape)],
        out_specs=tuple(out_specs),
        out_shape=tuple(out_shape),
        scratch_shapes=scratch,
        compiler_params=_cparams(1),
        name="inproj",
    )(x, gain, w_main, w_gate, qn, kn, bg, cos, sa, sb, ind, indt)


def _attn_body(q_ref, k_ref, v_ref, o_ref, acc_ref, m0_ref, m1_ref, l0_ref, l1_ref, *, seq):
    head0 = lax.broadcasted_iota(jnp.int32, (BAND, LANES), 1) < HEAD_DIM
    qrow = lax.broadcasted_iota(jnp.int32, (2 * BAND, 2 * BAND), 0) % BAND
    kcol = lax.broadcasted_iota(jnp.int32, (2 * BAND, 2 * BAND), 1)
    in_cur = kcol >= BAND
    qrow1 = lax.broadcasted_iota(jnp.int32, (2 * BAND, BAND), 0) % BAND
    kcol1 = lax.broadcasted_iota(jnp.int32, (2 * BAND, BAND), 1)

    for pi, dil in enumerate(DILATIONS):
        nb = seq // (dil * BAND)
        first, last = pi == 0, pi == len(DILATIONS) - 1
        grp = min(ATTN_GROUP, nb)
        nres = ATTN_GROUP // grp
        ngrp = nb // grp
        u = max(NRES // dil, 1)
        sl = BAND // u
        stride = max(dil // NRES, 1)
        assert not last or u == 1
        pos = lambda i, u=u, sl=sl: u * (i % sl) + i // sl
        dist = (BAND + pos(qrow)) - (BAND * (kcol // BAND) + pos(kcol % BAND))
        band = (dist >= 0) & (dist <= BAND)
        tri = pos(kcol1) <= pos(qrow1)

        def group(idx, carry, dil=dil, first=first, last=last, grp=grp, nres=nres, ngrp=ngrp, u=u, sl=sl,
                  stride=stride, band=band, tri=tri):
            rr, gi = (idx // ngrp, idx % ngrp) if ngrp > 1 else (idx, 0)

            def rows_of(r_d, blk, n):
                if dil <= NRES:
                    a = blk * sl
                    a = a if isinstance(a, int) else pl.multiple_of(a, 8)
                    return [(dil * c + r_d, pl.ds(a, n * sl)) for c in range(u)]
                return [(r_d % NRES, pl.ds(stride * BAND * blk + r_d // NRES, n * BAND, stride=stride))]

            def slabs(ref, r_d, blk, n):
                parts = [ref[i0, rows, :] for i0, rows in rows_of(r_d, blk, n)]
                return [jnp.concatenate([p[j * sl:(j + 1) * sl] for p in parts], axis=0) for j in range(n)]

            tasks = []
            for ri in range(nres):
                r_d = rr * nres + ri
                a0 = gi * grp
                kb = [x.astype(BF16) for x in slabs(k_ref.at[0], r_d, a0, grp)]
                vb = [x.astype(BF16) for x in slabs(v_ref.at[0], r_d, a0, grp)]
                if ngrp > 1:
                    ap = jnp.maximum(a0 - 1, 0)
                    kb.insert(0, slabs(k_ref.at[0], r_d, ap, 1)[0].astype(BF16))
                    vb.insert(0, slabs(v_ref.at[0], r_d, ap, 1)[0].astype(BF16))
                qs = slabs(q_ref.at[0], r_d, a0, grp)
                if not first:
                    olds = [slabs(ref, r_d, a0, grp) for ref in (m0_ref, m1_ref, l0_ref, l1_ref, acc_ref)]
                for j in range(grp):
                    jj = j + (1 if ngrp > 1 else 0)
                    old = None
                    if not first:
                        old = (jnp.concatenate([olds[0][j], olds[1][j]], axis=0),
                               jnp.concatenate([olds[2][j], olds[3][j]], axis=0), olds[4][j])
                    if jj == 0:
                        kk, vv, mask = kb[0], vb[0], tri
                    else:
                        kk = jnp.concatenate([kb[jj - 1], kb[jj]], axis=0)
                        vv = jnp.concatenate([vb[jj - 1], vb[jj]], axis=0)
                        mask = band & (in_cur | (gi > 0)) if (j == 0 and ngrp > 1) else band
                    tasks.append(((r_d, a0 + j), qs[j], kk, vv, mask, old))
            results = []
            for rows_j, q, kk, vv, mask, old in tasks:
                qq = jnp.concatenate([jnp.where(head0, q, 0.0), jnp.where(head0, 0.0, q)], axis=0).astype(BF16)
                s = jnp.where(mask, _dot_nt(qq, kk), NEG_INF)
                rm = jnp.max(s, axis=1, keepdims=True)
                if first:
                    mn = jnp.broadcast_to(rm, (2 * BAND, LANES))
                else:
                    mn = jnp.maximum(old[0], rm)
                p = jnp.exp(s - jnp.concatenate([mn] * (s.shape[1] // LANES), axis=1))
                pv = _dot(p.astype(BF16), jnp.concatenate([vv, jnp.ones_like(vv)], axis=1))
                ls = pv[:, LANES:]
                acc = jnp.where(head0, pv[:BAND, :LANES], pv[BAND:, :LANES])
                if first:
                    ln = ls
                else:
                    a = jnp.exp(old[0] - mn)
                    ln = a * old[1] + ls
                    acc = jnp.where(head0, a[:BAND], a[BAND:]) * old[2] + acc
                results.append((rows_j, mn, ln, acc))
            for (r_d, blk), mn, ln, acc in results:
                if last:
                    o_ref[0, pl.ds(dil * BAND * blk + r_d, BAND, stride=dil), :] = (
                        acc / jnp.where(head0, ln[:BAND], ln[BAND:]))
                else:
                    for ref, val in ((m0_ref, mn[:BAND]), (m1_ref, mn[BAND:]), (l0_ref, ln[:BAND]),
                                     (l1_ref, ln[BAND:]), (acc_ref, acc)):
                        for c, (i0, rows) in enumerate(rows_of(r_d, blk, 1)):
                            ref[i0, rows, :] = val[c * sl:(c + 1) * sl]
            return carry

        lax.fori_loop(0, (dil * nb) // ATTN_GROUP, group, 0)


def _attention(q, k, v):
    b, _, la, wa = q.shape
    s = la * NRES
    spec = pl.BlockSpec((1, NRES, la, LANES), lambda i, j: (i, 0, 0, j))
    scratch = [pltpu.VMEM((NRES, la, LANES), F32) for _ in range(5)]
    return pl.pallas_call(
        functools.partial(_attn_body, seq=s),
        grid=(b, wa // LANES),
        in_specs=[spec, spec, spec],
        out_specs=pl.BlockSpec((1, s, LANES), lambda i, j: (i, 0, j)),
        out_shape=jax.ShapeDtypeStruct((b, s, wa), F32),
        scratch_shapes=scratch,
        compiler_params=_cparams(2),
        name="dilated_attn",
    )(q, k, v)


def _mlstm_body(q_ref, k_ref, v_ref, g_ref, h_ref, c_ref, n_ref, m_ref, cst_ref, nst_ref, mst_ref, gs_ref, *,
                nchunks, nheads):
    L = CHUNK
    npairs = nheads // 2
    row = lax.broadcasted_iota(jnp.int32, (L, L), 0)
    lane = lax.broadcasted_iota(jnp.int32, (L, L), 1)
    lane_h = lax.broadcasted_iota(jnp.int32, (nheads, L), 1)
    causal_t = row <= lane
    head0_lane = lane < HEAD_DIM
    head0_row = row < HEAD_DIM
    blockdiag = head0_row == head0_lane
    lane8 = lax.broadcasted_iota(jnp.int32, (8, L), 1)
    row8 = lax.broadcasted_iota(jnp.int32, (8, L), 0)
    pad = jnp.zeros((L - 2 * nheads, L), F32)

    @pl.when(pl.program_id(1) == 0)
    def _():
        cst_ref[...] = jnp.zeros_like(cst_ref)
        nst_ref[...] = jnp.zeros_like(nst_ref)
        mst_ref[...] = jnp.zeros_like(mst_ref)

    def scan_lanes(x, op, fill):
        sh = 1
        while sh < L:
            x = op(x, jnp.where(lane_h >= sh, pltpu.roll(x, sh, 1), fill))
            sh *= 2
        return x

    m_prev = mst_ref[...]
    for c in range(nchunks):
        sl = slice(c * L, (c + 1) * L)
        b_c = scan_lanes(g_ref[0, nheads:2 * nheads, sl], jnp.add, 0.0)
        g_c = g_ref[0, 0:nheads, sl] - b_c
        cm_c = scan_lanes(g_c, jnp.maximum, NEG_INF)
        mm_c = jnp.maximum(cm_c, m_prev)
        mm_last = jnp.maximum(jnp.broadcast_to(cm_c[:, L - 1:L], (nheads, L)), m_prev)
        gs_ref[0, :, sl] = g_c
        gs_ref[1, :, sl] = mm_c
        gs_ref[2, :, sl] = jnp.exp(m_prev - mm_c)
        gs_ref[3, :, sl] = jnp.exp(-(b_c + mm_c))
        gs_ref[4, :, sl] = jnp.exp(g_c - mm_last)
        m_prev = jnp.broadcast_to(b_c[:, L - 1:L], (nheads, L)) + mm_last
    mst_ref[...] = m_prev

    def chunk(c, carry):
        t0 = pl.multiple_of(c * L, L)
        g_all = gs_ref[0, :, pl.ds(t0, L)]
        mm_all = gs_ref[1, :, pl.ds(t0, L)]
        iw_all = gs_ref[2, :, pl.ds(t0, L)]
        en_all = gs_ref[3, :, pl.ds(t0, L)]
        ws_all = gs_ref[4, :, pl.ds(t0, L)]
        cols = jnp.concatenate([g_all, ws_all, pad], axis=0).T
        for p in range(npairs):
            lanes = slice(p * LANES, (p + 1) * LANES)
            h0, h1 = 2 * p, 2 * p + 1
            q = q_ref[0, pl.ds(t0, L), lanes]
            k = k_ref[0, pl.ds(t0, L), lanes]
            v = v_ref[0, pl.ds(t0, L), lanes]
            zero = jnp.zeros_like(q)
            qq = jnp.concatenate([jnp.where(head0_lane, q, zero), jnp.where(head0_lane, zero, q)], axis=0)
            qk = _dot_nt(k, qq)
            cst = cst_ref[p]
            nst = nst_ref[p]
            sc, rs = [], []
            for h, hh in enumerate((h0, h1)):
                d = jnp.where(causal_t, cols[:, hh:hh + 1] - mm_all[hh:hh + 1, :], NEG_INF)
                s_h = qk[:, h * L:(h + 1) * L] * jnp.exp(d)
                sc.append(s_h.astype(BF16))
                rs.append(jnp.sum(s_h, axis=0, keepdims=True))
            ws_l = jnp.where(head0_lane, cols[:, nheads + h0:nheads + h0 + 1], cols[:, nheads + h1:nheads + h1 + 1])
            kws = k.astype(F32) * ws_l
            u = _dot_tn(v, jnp.concatenate(sc + [kws.astype(BF16)], axis=1))
            intra = jnp.where(head0_row, u[:, :L], u[:, L:2 * L])
            inter = _dot_nt(cst.astype(BF16), q)
            qn = _dot_nt(nst.astype(BF16), q)
            iw_l = jnp.where(head0_row, iw_all[h0:h0 + 1, :], iw_all[h1:h1 + 1, :])
            den0 = iw_all[h0:h0 + 1, :] * qn[0:1, :] + rs[0]
            den1 = iw_all[h1:h1 + 1, :] * qn[1:2, :] + rs[1]
            lim = jnp.maximum(jnp.abs(jnp.where(head0_row, den0, den1)),
                              jnp.where(head0_row, en_all[h0:h0 + 1, :], en_all[h1:h1 + 1, :]))
            h_ref[0, pl.ds(t0, L), lanes] = ((iw_l * inter + intra) / lim).T

            cw0 = iw_all[h0:h0 + 1, L - 1:L]
            cw1 = iw_all[h1:h1 + 1, L - 1:L]
            cst_ref[p] = jnp.where(head0_row, cw0, cw1) * cst + jnp.where(blockdiag, u[:, 2 * L:], 0.0)
            ksum = jnp.sum(kws, axis=0, keepdims=True)
            own = (row8 == 0) == (lane8 < HEAD_DIM)
            nst_ref[p] = jnp.where((row8 < 2) & own, jnp.where(row8 == 0, cw0, cw1) * nst + ksum, 0.0)
        return carry

    lax.fori_loop(0, nchunks, chunk, 0, unroll=2)

    @pl.when(pl.program_id(1) == pl.num_programs(1) - 1)
    def _():
        for p in range(npairs):
            for h in range(2):
                sl = slice(h * HEAD_DIM, (h + 1) * HEAD_DIM)
                c_ref[0, 2 * p + h] = cst_ref[p, sl, sl]
                n_ref[0, 2 * p + h:2 * p + h + 1, :] = nst_ref[p, h:h + 1, sl]
        m_ref[0] = mst_ref[...]


def _mlstm(q, k, v, gates, *, ts):
    b, s, wb = q.shape
    nheads = wb // HEAD_DIM
    spec = pl.BlockSpec((1, ts, wb), lambda i, j: (i, j, 0))
    out_shape = (jax.ShapeDtypeStruct((b, s, wb), F32),
                 jax.ShapeDtypeStruct((b, nheads, HEAD_DIM, HEAD_DIM), F32),
                 jax.ShapeDtypeStruct((b, nheads, HEAD_DIM), F32),
                 jax.ShapeDtypeStruct((b, nheads, LANES), F32))
    out_specs = (spec,
                 pl.BlockSpec((1, nheads, HEAD_DIM, HEAD_DIM), lambda i, j: (i, 0, 0, 0)),
                 pl.BlockSpec((1, nheads, HEAD_DIM), lambda i, j: (i, 0, 0)),
                 pl.BlockSpec((1, nheads, LANES), lambda i, j: (i, 0, 0)))
    return pl.pallas_call(
        functools.partial(_mlstm_body, nchunks=ts // CHUNK, nheads=nheads),
        grid=(b, s // ts),
        in_specs=[spec, spec, spec, pl.BlockSpec((1, 2 * nheads, ts), lambda i, j: (i, 0, j))],
        out_specs=out_specs,
        out_shape=out_shape,
        scratch_shapes=[pltpu.VMEM((nheads // 2, LANES, LANES), F32), pltpu.VMEM((nheads // 2, 8, LANES), F32),
                        pltpu.VMEM((nheads, LANES), F32), pltpu.VMEM((5, nheads, ts), F32)],
        compiler_params=_cparams(2),
        name="mlstm",
    )(q, k, v, gates)


def _post_body(*refs, prompt, seq_tiles, ck):
    (x_ref, oa_ref, hb_ref, og_ref, p_ref, ga_ref, gb_ref, wo_ref, g_ref, wup_ref, cw_ref, cb_ref, wdn_ref,
     gp_ref, wpg_ref, wpp_ref) = refs[:16]
    if prompt:
        o_ref, conv_ref, y_ref, carry_ref = refs[16:]
    else:
        p0_ref, p1_ref, o_ref, conv_ref, y_ref = refs[16:]
    tm = x_ref.shape[0]
    dff = wdn_ref.shape[0]
    wa = oa_ref.shape[1]
    ya = _rms(oa_ref[...], ga_ref[...]).astype(BF16)
    yb = (og_ref[...] * _rms(hb_ref[...], gb_ref[...])).astype(BF16)
    x = x_ref[...] + _dot(ya, wo_ref[0:wa, :]) + _dot(yb, wo_ref[wa:, :])
    h = _rms(x, g_ref[...]).astype(BF16)
    if prompt:
        @pl.when(pl.program_id(0) % seq_tiles == 0)
        def _():
            carry_ref[...] = jnp.zeros_like(carry_ref)
        row = lax.broadcasted_iota(jnp.int32, (tm, ck), 0)
    for c in range(dff // ck):
        cs = slice(c * ck, (c + 1) * ck)
        gate = _dot(h, wup_ref[:, c * ck:(c + 1) * ck])
        val = _dot(h, wup_ref[:, dff + c * ck:dff + (c + 1) * ck])
        if prompt:
            b0 = carry_ref[0:1, cs]
            b1 = carry_ref[1:2, cs]
            g1 = jnp.where(row == 0, b1, pltpu.roll(gate, 1, 0))
            g2 = jnp.where(row == 0, b0, jnp.where(row == 1, b1, pltpu.roll(gate, 2, 0)))
            carry_ref[:, cs] = gate[tm - 2:tm, :]
        else:
            g2 = p0_ref[:, cs]
            g1 = p1_ref[:, cs]
            conv_ref[:, cs] = gate
        gc = cb_ref[:, cs] + (cw_ref[0:1, cs] * g2 + cw_ref[1:2, cs] * g1 + cw_ref[2:3, cs] * gate)
        y_ref[:, cs] = (_gelu_tanh(gc) * val).astype(BF16)
    if prompt:
        conv_ref[0] = carry_ref[...]
    x = x + _dot(y_ref[...], wdn_ref[...])
    gate = jax.nn.sigmoid(_dot(_rms(x, gp_ref[...]).astype(BF16), wpg_ref[...]))
    o_ref[...] = x + gate * _dot(p_ref[...].astype(BF16), wpp_ref[...])


def _post(x, oa, hb, og, p, layer, params, prev=None, *, tm, seq_tiles=1, batch=1):
    t, d = x.shape
    dff = params[7].shape[1]
    prompt = prev is None
    tok = lambda w_: pl.BlockSpec((tm, w_), lambda i: (i, 0))
    in_specs = [tok(d), tok(oa.shape[1]), tok(hb.shape[1]), tok(og.shape[1]),
                pl.BlockSpec((None, tm, p.shape[2]), lambda i: (layer, i, 0))]
    in_specs += [_layer_spec(a, layer) for a in params]
    args = [x, oa, hb, og, p, *params]
    scratch = [pltpu.VMEM((tm, dff), BF16)]
    if prompt:
        conv_shape = jax.ShapeDtypeStruct((batch, 2, dff), F32)
        conv_spec = pl.BlockSpec((1, 2, dff), lambda i: (i // seq_tiles, 0, 0))
        scratch.append(pltpu.VMEM((2, dff), F32))
    else:
        in_specs += [tok(dff), tok(dff)]
        args += list(prev)
        conv_shape = jax.ShapeDtypeStruct((t, dff), F32)
        conv_spec = tok(dff)
    return pl.pallas_call(
        functools.partial(_post_body, prompt=prompt, seq_tiles=seq_tiles, ck=FFN_CHUNK),
        grid=(t // tm,),
        in_specs=in_specs,
        out_specs=(tok(d), conv_spec),
        out_shape=(jax.ShapeDtypeStruct((t, d), F32), conv_shape),
        scratch_shapes=scratch,
        compiler_params=_cparams(1),
        name="post_mixer",
    )(*args)


def _window_body(k_ref, v_ref, kacc_ref, vacc_ref, ko_ref, vo_ref):
    del kacc_ref, vacc_ref
    ko_ref[...] = k_ref[0].T
    vo_ref[...] = v_ref[0].T


def _window_rows(k, v, kacc, vacc, layer, *, keep, tpos):
    b, s, w = k.shape
    first = (s - keep) // tpos
    src = pl.BlockSpec((1, tpos, w), lambda i, j: (i, first + j, 0))
    acc = pl.BlockSpec(memory_space=pl.ANY)
    dst = pl.BlockSpec((None, None, w, tpos), lambda i, j: (layer, i, 0, j))
    return pl.pallas_call(
        _window_body,
        grid=(b, keep // tpos),
        in_specs=[src, src, acc, acc],
        out_specs=(dst, dst),
        out_shape=(jax.ShapeDtypeStruct(kacc.shape, F32), jax.ShapeDtypeStruct(vacc.shape, F32)),
        input_output_aliases={2: 0, 3: 1},
        compiler_params=_cparams(2),
        name="window_rows",
    )(k, v, kacc, vacc)


def _sattn_body(q_ref, kn_ref, vn_ref, kt_ref, vt_ref, o_ref):
    q = q_ref[0]
    kn = kn_ref[0]
    vn = vn_ref[0]
    nh, _, wbuf = kt_ref.shape
    r = lax.broadcasted_iota(jnp.int32, (HEAD_DIM, HEAD_DIM), 0)
    c = lax.broadcasted_iota(jnp.int32, (HEAD_DIM, HEAD_DIM), 1)
    diag = r == c
    s_new = jnp.sum(q * kn, axis=-1, keepdims=True)
    rows = []
    for h in range(nh):
        q_col = jnp.sum(jnp.where(diag, q[h:h + 1, :], 0.0), axis=1, keepdims=True)
        rows.append(jnp.sum(kt_ref[h] * q_col, axis=0, keepdims=True))
    s = jnp.concatenate(rows, axis=0)
    dist = wbuf - lax.broadcasted_iota(jnp.int32, (1, wbuf), 1)
    stats = []
    for dil in DILATIONS:
        mask = ((dist & (dil - 1)) == 0) & (dist <= BAND * dil)
        sg = jnp.where(mask, s, NEG_INF)
        m = jnp.maximum(jnp.max(sg, axis=1, keepdims=True), s_new)
        e = jnp.exp(sg - m)
        e_new = jnp.exp(s_new - m)
        stats.append((m, e, e_new, jnp.sum(e, axis=1, keepdims=True) + e_new))
    mm = jnp.maximum(jnp.maximum(stats[0][0], stats[1][0]), stats[2][0])
    cs = [jnp.exp(m - mm) for m, _, _, _ in stats]
    wgt = cs[0] * stats[0][1] + cs[1] * stats[1][1] + cs[2] * stats[2][1]
    w_new = cs[0] * stats[0][2] + cs[1] * stats[1][2] + cs[2] * stats[2][2]
    den = cs[0] * stats[0][3] + cs[1] * stats[1][3] + cs[2] * stats[2][3]
    outs = []
    for h in range(nh):
        o_col = jnp.sum(vt_ref[h] * wgt[h:h + 1, :], axis=1, keepdims=True)
        outs.append(jnp.sum(jnp.where(diag, o_col, 0.0), axis=0, keepdims=True))
    o_ref[0] = (jnp.concatenate(outs, axis=0) + w_new * vn) / den


def _sample_attention(q, kn, vn, ckt, cvt, layer):
    b, nh, dh = q.shape
    wbuf = ckt.shape[-1]
    row = pl.BlockSpec((1, nh, dh), lambda i: (i, 0, 0))
    cache = pl.BlockSpec((None, None, nh, dh, wbuf), lambda i: (layer, i, 0, 0, 0))
    return pl.pallas_call(
        _sattn_body,
        grid=(b,),
        in_specs=[row, row, row, cache, cache],
        out_specs=row,
        out_shape=jax.ShapeDtypeStruct((b, nh, dh), F32),
        compiler_params=_cparams(1),
        name="sample_attn",
    )(q, kn, vn, ckt, cvt)


def _smlstm_body(q_ref, k_ref, v_ref, ig_ref, lf_ref, c_ref, n_ref, m_ref, h_ref, co_ref, no_ref, mo_ref):
    q, k, v = q_ref[0], k_ref[0], v_ref[0]
    ig, lf, m = ig_ref[0], lf_ref[0], m_ref[0]
    n = n_ref[0]
    nh = q.shape[0]
    inter = lf + m
    m_t = jnp.maximum(inter, ig)
    dw = jnp.exp(ig - m_t)
    iw = jnp.exp(inter - m_t)
    sc = jnp.sum(q * k, axis=-1, keepdims=True) * dw
    r = lax.broadcasted_iota(jnp.int32, (HEAD_DIM, HEAD_DIM), 0)
    l = lax.broadcasted_iota(jnp.int32, (HEAD_DIM, HEAD_DIM), 1)
    cq = []
    for h in range(nh):
        c_h = c_ref[0, h]
        q_h = jnp.broadcast_to(q[h:h + 1, :], (8, HEAD_DIM)).astype(BF16)
        cq.append(_dot_nt(q_h, c_h.astype(BF16))[0:1, :])
        v_col = jnp.sum(jnp.where(r == l, v[h:h + 1, :], 0.0), axis=1, keepdims=True)
        co_ref[0, h] = iw[h:h + 1, :] * c_h + (dw[h:h + 1, :] * v_col) * k[h:h + 1, :]
    num = iw * jnp.concatenate(cq, axis=0) + sc * v
    den = iw * jnp.sum(n * q, axis=-1, keepdims=True) + sc
    h_ref[0] = num / jnp.maximum(jnp.abs(den), jnp.exp(-m_t))
    no_ref[0] = iw * n + dw * k
    mo_ref[0] = m_t


def _sample_mlstm(q, k, v, ig, lf, c, n, m):
    b, nh, dh = q.shape
    row = pl.BlockSpec((1, nh, dh), lambda i: (i, 0, 0))
    col = pl.BlockSpec((1, nh, 1), lambda i: (i, 0, 0))
    cst = pl.BlockSpec((1, nh, dh, dh), lambda i: (i, 0, 0, 0))
    return pl.pallas_call(
        _smlstm_body,
        grid=(b,),
        in_specs=[row, row, row, col, col, cst, row, col],
        out_specs=(row, cst, row, col),
        out_shape=(jax.ShapeDtypeStruct((b, nh, dh), F32), jax.ShapeDtypeStruct((b, nh, dh, dh), F32),
                   jax.ShapeDtypeStruct((b, nh, dh), F32), jax.ShapeDtypeStruct((b, nh, 1), F32)),
        compiler_params=_cparams(1),
        name="sample_mlstm",
    )(q, k, v, ig, lf, c, n, m)


def _rope_tables(pos):
    half = ROT_DIM // 2
    inv = ROPE_THETA ** (-jnp.arange(half, dtype=F32) / half)
    ang = pos.astype(F32)[:, None] * inv[None, :]
    cos, sin = jnp.cos(ang), jnp.sin(ang)
    n = pos.shape[0]
    rest = HEAD_DIM - ROT_DIM
    c = jnp.concatenate([cos, cos, jnp.ones((n, rest), F32)], axis=1)
    sa = jnp.concatenate([-sin, jnp.zeros((n, half + rest), F32)], axis=1)
    sb = jnp.concatenate([jnp.zeros((n, half), F32), sin, jnp.zeros((n, rest), F32)], axis=1)
    rep = LANES // HEAD_DIM
    return tuple(jnp.concatenate([t] * rep, axis=1) for t in (c, sa, sb))


def kernel(x_prompt, x_sample, cache_win_k, cache_win_v, state_C, state_n, state_m, state_conv, p_prompt, p_sample,
           norm_mix, w_in, q_norm, k_norm, b_gates, out_norm_a, out_norm_b, w_out, norm_ffn, w_up, conv_w, conv_b,
           w_down, norm_ple, w_ple_gate, w_ple_proj):
    batch, seq, d = x_prompt.shape
    dec_batch, dec_seq, _ = x_sample.shape
    depth = w_in.shape[0]
    nh_a = cache_win_k.shape[3]
    nh_b = state_C.shape[2]
    wa, wb = nh_a * HEAD_DIM, nh_b * HEAD_DIM
    wbuf = cache_win_k.shape[2]
    win_max = BAND * DILATIONS[-1]
    keep = min(win_max, seq)
    assert dec_seq == 1 and wbuf == win_max and seq % win_max == 0
    assert w_in.shape[2] == 3 * wa + 4 * wb + 2 * nh_b and wa % LANES == 0 and wb % LANES == 0

    tm = min(TOKEN_TILE, seq)
    seq_tiles = seq // tm
    tp, ts = batch * seq, dec_batch * dec_seq

    cut = 3 * wa + 4 * wb
    w_main = w_in[:, :, :cut].astype(BF16)
    w_gate = w_in[:, :, cut + 2 * nh_b - LANES:].astype(BF16)
    bg = jnp.concatenate([jnp.zeros((depth, GATE_LANE0), F32), b_gates], axis=1)[:, None, :]
    qn = jnp.tile(q_norm, (1, nh_a))[:, None, :]
    kn = jnp.tile(k_norm, (1, nh_a))[:, None, :]
    rows = lambda a: a[:, None, :]
    inproj_params = (rows(norm_mix), w_main, w_gate, qn, kn, bg)
    post_params = (rows(out_norm_a), rows(out_norm_b), w_out.astype(BF16), rows(norm_ffn), w_up.astype(BF16),
                   conv_w, rows(conv_b), w_down.astype(BF16), rows(norm_ple), w_ple_gate.astype(BF16),
                   w_ple_proj.astype(BF16))
    pp = p_prompt.reshape(depth, tp, -1)
    ps = p_sample.reshape(depth, ts, -1)
    head_of_lane = jnp.arange(wa) // HEAD_DIM
    ind = (head_of_lane[:, None] == jnp.arange(LANES)[None, :]).astype(BF16)
    indt = ind.T
    tab_p = _rope_tables(jnp.arange(seq, dtype=jnp.int32))
    tab_s = tuple(jnp.broadcast_to(t, (ts, LANES))
                  for t in _rope_tables(PAST_LEN + jnp.arange(dec_seq, dtype=jnp.int32)))

    ckt = jnp.transpose(cache_win_k, (0, 1, 3, 4, 2))
    cvt = jnp.transpose(cache_win_v, (0, 1, 3, 4, 2))

    xp = x_prompt.reshape(tp, d)
    xs = x_sample.reshape(ts, d)
    outs = [[] for _ in range(12)]
    wk_acc = jnp.zeros((depth, batch, wa, keep), F32)
    wv_acc = jnp.zeros_like(wk_acc)
    for i in range(depth):
        q_rm, ka, va, qb, kb, vb, og, gt, k_rm, v_rm = _inproj(
            xp, i, *inproj_params, *tab_p, ind, indt,
            tm=tm, seq_tiles=seq_tiles, batch=batch, mdtype=BF16, transpose_gates=True)
        r3 = lambda a: a.reshape(batch, seq, a.shape[-1])
        oa = _attention(q_rm, k_rm, v_rm)
        hb, c_new, n_new, m_new = _mlstm(r3(qb), r3(kb), r3(vb), gt, ts=min(seq, MLSTM_TILE))
        xp, conv_new = _post(xp, oa.reshape(tp, wa), hb.reshape(tp, wb), og, pp, i, post_params,
                             tm=tm, seq_tiles=seq_tiles, batch=batch)
        wk_acc, wv_acc = _window_rows(r3(ka), r3(va), wk_acc, wv_acc, i, keep=keep, tpos=tm)
        outs[2].append(c_new)
        outs[3].append(n_new)
        outs[4].append(m_new[..., 0])
        outs[5].append(conv_new)
        qa, ka, va, qb, kb, vb, og, ga = _inproj(
            xs, i, *inproj_params, *tab_s, ind, indt,
            tm=ts, seq_tiles=1, batch=dec_batch, mdtype=F32, transpose_gates=False)
        hd = lambda a: a.reshape(dec_batch, -1, HEAD_DIM)
        oa = _sample_attention(hd(qa), hd(ka), hd(va), ckt, cvt, i)
        ig = ga[:, GATE_LANE0:GATE_LANE0 + nh_b, None]
        lf = ga[:, GATE_LANE0 + nh_b:, None]
        hb, c_new, n_new, m_new = _sample_mlstm(hd(qb), hd(kb), hd(vb), ig, lf, state_C[i], state_n[i],
                                                state_m[i][..., None])
        xs, gate_new = _post(xs, oa.reshape(ts, wa), hb.reshape(ts, wb), og, ps, i, post_params,
                             prev=(state_conv[i][:, 0], state_conv[i][:, 1]), tm=ts)
        outs[6].append(ka.reshape(dec_batch, dec_seq, nh_a, HEAD_DIM))
        outs[7].append(va.reshape(dec_batch, dec_seq, nh_a, HEAD_DIM))
        outs[8].append(c_new)
        outs[9].append(n_new)
        outs[10].append(m_new[..., 0])
        outs[11].append(jnp.stack([state_conv[i][:, 1], gate_new], axis=1))
    def window_out(acc):
        return jnp.transpose(acc.reshape(depth, batch, nh_a, HEAD_DIM, keep), (0, 1, 4, 2, 3))

    st = jnp.stack
    return ((xp.reshape(batch, seq, d), xs.reshape(dec_batch, dec_seq, d), window_out(wk_acc), window_out(wv_acc))
            + tuple(st(o) for o in outs[2:]))
```

```python
import functools

import jax
import jax.numpy as jnp
from jax import lax
from jax.experimental import pallas as pl
from jax.experimental.pallas import tpu as pltpu

F32 = jnp.float32
BF16 = jnp.bfloat16

HEAD_DIM = 64
ROT_DIM = HEAD_DIM // 4
ROPE_THETA = 500000.0
RMS_EPS = 1e-6
NEG_INF = -1e30
PAST_LEN = 8192
DILATIONS = (1, 4, 16)
NRES = 4
BAND = 128
ATTN_GROUP = 8
CHUNK = 128
FFN_CHUNK = 256
TOKEN_TILE = 512
MLSTM_TILE = 1024
LANES = 128
GATE_LANE0 = LANES - 16
VMEM_LIMIT = 56 * 1024 * 1024


def _cparams(n_axes):
    return pltpu.CompilerParams(dimension_semantics=("arbitrary",) * n_axes,
                                vmem_limit_bytes=VMEM_LIMIT)


def _const_spec(shape):
    return pl.BlockSpec(shape, lambda *_: (0,) * len(shape))


def _layer_spec(stack, layer):
    nd = stack.ndim
    return pl.BlockSpec((None,) + stack.shape[1:], lambda *_: (layer,) + (0,) * (nd - 1),
                        pipeline_mode=pl.Buffered(1))


def _rms(x, g):
    return (x * lax.rsqrt(jnp.mean(x * x, axis=-1, keepdims=True) + RMS_EPS)) * g


def _dot(a, b):
    return jnp.dot(a, b, preferred_element_type=F32)


def _dot_nt(a, b):
    return lax.dot_general(a, b, (((1,), (1,)), ((), ())), preferred_element_type=F32)


def _dot_tn(a, b):
    return lax.dot_general(a, b, (((0,), (0,)), ((), ())), preferred_element_type=F32)


def _split_dot(a, b):
    hi = a.astype(BF16)
    lo = (a - hi.astype(F32)).astype(BF16)
    return _dot(hi, b) + _dot(lo, b)


def _log_sigmoid(x):
    return jnp.minimum(x, 0.0) - jnp.log1p(jnp.exp(-jnp.abs(x)))


def _gelu_tanh(x):
    c = 0.7978845608028654
    return x * (0.5 * (1.0 + jnp.tanh(c * (x + 0.044715 * (x * x * x)))))


def _inproj_body(x_ref, g_ref, w_ref, wg_ref, qn_ref, kn_ref, bg_ref, *refs, wa, transpose_gates):
    h = _rms(x_ref[...], g_ref[...]).astype(BF16)
    reps = wa // LANES
    half = ROT_DIM // 2
    if transpose_gates:
        cos_ref, sin_ref = refs[:2]
        qa_ref, ka_ref, va_ref, qb_ref, kb_ref, vb_ref, og_ref, gt_ref, *rest = refs[2:]
        tiles = x_ref.shape[0] // LANES

        def head_norm_rope(z, gain_ref):
            zt = z.T
            cos, sin = cos_ref[...], sin_ref[...]
            heads = []
            for hh in range(wa // HEAD_DIM):
                x = zt[hh * HEAD_DIM:(hh + 1) * HEAD_DIM, :]
                gain = jnp.concatenate([gain_ref[hh * HEAD_DIM:(hh + 1) * HEAD_DIM, :]] * tiles, axis=1)
                xn = (x * lax.rsqrt(jnp.mean(x * x, axis=0, keepdims=True) + RMS_EPS)) * gain
                x1, x2 = xn[0:half], xn[half:2 * half]
                heads.append(jnp.concatenate([x1 * cos - x2 * sin, x2 * cos + x1 * sin, xn[2 * half:]], axis=0))
            return jnp.concatenate(heads, axis=0).T
    else:
        cos_ref, sa_ref, sb_ref, ind_ref, indt_ref = refs[:5]
        qa_ref, ka_ref, va_ref, qb_ref, kb_ref, vb_ref, og_ref, gt_ref, *rest = refs[5:]
        cos = jnp.concatenate([cos_ref[...]] * reps, axis=1)
        sa = jnp.concatenate([sa_ref[...]] * reps, axis=1)
        sb = jnp.concatenate([sb_ref[...]] * reps, axis=1)
        ind = ind_ref[...]
        indt = indt_ref[...]

        def head_norm_rope(z, gain_ref):
            ss = _split_dot(z * z, ind)
            inv = lax.rsqrt(ss * (1.0 / HEAD_DIM) + RMS_EPS)
            zn = (z * _split_dot(inv, indt)) * gain_ref[...]
            return zn * cos + pltpu.roll(zn, wa - half, 1) * sa + pltpu.roll(zn, half, 1) * sb

    qa = head_norm_rope(_dot(h, w_ref[:, 0:wa]), qn_ref) * (HEAD_DIM ** -0.5)
    ka = head_norm_rope(_dot(h, w_ref[:, wa:2 * wa]), kn_ref)
    va = _dot(h, w_ref[:, 2 * wa:3 * wa])
    ka_ref[...] = ka
    va_ref[...] = va
    if transpose_gates:
        krm_ref, vrm_ref, zs_ref = rest
        rows = zs_ref.shape[1] // NRES
        for val, dst in ((qa, qa_ref), (ka, krm_ref), (va, vrm_ref)):
            for c in range(reps):
                zs_ref[c] = val[:, c * LANES:(c + 1) * LANES]
            for r in range(NRES):
                for c in range(reps):
                    dst[0, r, :, c * LANES:(c + 1) * LANES] = zs_ref[c, pl.ds(r, rows, stride=NRES), :]
    else:
        qa_ref[...] = qa
    wb = (w_ref.shape[1] - 3 * wa) // 4
    o = 3 * wa
    qb_ref[...] = _dot(h, w_ref[:, o:o + wb]).astype(qb_ref.dtype)
    kb_ref[...] = (_dot(h, w_ref[:, o + wb:o + 2 * wb]) * (HEAD_DIM ** -0.5)).astype(kb_ref.dtype)
    vb_ref[...] = _dot(h, w_ref[:, o + 2 * wb:o + 3 * wb]).astype(vb_ref.dtype)
    og_ref[...] = jax.nn.sigmoid(_dot(h, w_ref[:, o + 3 * wb:o + 4 * wb]))

    v = _dot(h, wg_ref[...]) + bg_ref[...]
    lane = lax.broadcasted_iota(jnp.int32, v.shape, 1)
    ga = jnp.where(lane >= GATE_LANE0 + 8, _log_sigmoid(v), v)
    if transpose_gates:
        gt_ref[0] = ga.T[GATE_LANE0:, :]
    else:
        gt_ref[...] = ga


def _inproj(x, layer, gain, w_main, w_gate, qn, kn, bg, tables, consts, *, tm, seq_tiles, batch, mdtype,
            transpose_gates):
    t, d = x.shape
    wa = qn.shape[1] if transpose_gates else qn.shape[2]
    wb = (w_main.shape[2] - 3 * wa) // 4
    nt = t // tm
    tok = lambda w: pl.BlockSpec((tm, w), lambda i: (i, 0))
    if transpose_gates:
        tab = pl.BlockSpec((tables[0].shape[0], tm), lambda i: (0, i % seq_tiles))
    else:
        tab = pl.BlockSpec((tm, LANES), lambda i: (i % seq_tiles, 0))
    if transpose_gates:
        seq = seq_tiles * tm
        gt_shape = jax.ShapeDtypeStruct((batch, 16, seq), F32)
        gt_spec = pl.BlockSpec((1, 16, tm), lambda i: (i // seq_tiles, 0, i % seq_tiles))
    else:
        gt_shape = jax.ShapeDtypeStruct((t, LANES), F32)
        gt_spec = tok(LANES)
    out_shape = [
        jax.ShapeDtypeStruct((t, wa), F32), jax.ShapeDtypeStruct((t, wa), F32), jax.ShapeDtypeStruct((t, wa), F32),
        jax.ShapeDtypeStruct((t, wb), mdtype), jax.ShapeDtypeStruct((t, wb), mdtype),
        jax.ShapeDtypeStruct((t, wb), mdtype), jax.ShapeDtypeStruct((t, wb), F32), gt_shape]
    out_specs = [tok(wa), tok(wa), tok(wa), tok(wb), tok(wb), tok(wb), tok(wb), gt_spec]
    scratch = []
    if transpose_gates:
        rm_shape = jax.ShapeDtypeStruct((batch, NRES, seq // NRES, wa), F32)
        rm_spec = pl.BlockSpec((1, NRES, tm // NRES, wa), lambda i: (i // seq_tiles, 0, i % seq_tiles, 0))
        out_shape = [rm_shape] + out_shape[1:] + [rm_shape, rm_shape]
        out_specs = [rm_spec] + out_specs[1:] + [rm_spec, rm_spec]
        scratch = [pltpu.VMEM((wa // LANES, tm, LANES), F32)]
    return pl.pallas_call(
        functools.partial(_inproj_body, wa=wa, transpose_gates=transpose_gates),
        grid=(nt,),
        in_specs=[tok(d)] + [_layer_spec(a, layer) for a in (gain, w_main, w_gate, qn, kn, bg)]
                 + [tab] * len(tables) + [_const_spec(c.shape) for c in consts],
        out_specs=tuple(out_specs),
        out_shape=tuple(out_shape),
        scratch_shapes=scratch,
        compiler_params=_cparams(1),
        name="inproj",
    )(x, gain, w_main, w_gate, qn, kn, bg, *tables, *consts)


def _attn_body(q_ref, k_ref, v_ref, o_ref, acc_ref, m0_ref, m1_ref, l0_ref, l1_ref, *, seq):
    head0 = lax.broadcasted_iota(jnp.int32, (BAND, LANES), 1) < HEAD_DIM
    qrow = lax.broadcasted_iota(jnp.int32, (2 * BAND, 2 * BAND), 0) % BAND
    kcol = lax.broadcasted_iota(jnp.int32, (2 * BAND, 2 * BAND), 1)
    in_cur = kcol >= BAND
    qrow1 = lax.broadcasted_iota(jnp.int32, (2 * BAND, BAND), 0) % BAND
    kcol1 = lax.broadcasted_iota(jnp.int32, (2 * BAND, BAND), 1)

    for pi, dil in enumerate(DILATIONS):
        nb = seq // (dil * BAND)
        first, last = pi == 0, pi == len(DILATIONS) - 1
        grp = min(ATTN_GROUP, nb)
        nres = ATTN_GROUP // grp
        ngrp = nb // grp
        u = max(NRES // dil, 1)
        sl = BAND // u
        stride = max(dil // NRES, 1)
        assert not last or u == 1
        pos = lambda i, u=u, sl=sl: u * (i % sl) + i // sl
        dist = (BAND + pos(qrow)) - (BAND * (kcol // BAND) + pos(kcol % BAND))
        band = (dist >= 0) & (dist <= BAND)
        tri = pos(kcol1) <= pos(qrow1)

        def group(idx, carry, dil=dil, first=first, last=last, grp=grp, nres=nres, ngrp=ngrp, u=u, sl=sl,
                  stride=stride, band=band, tri=tri):
            rr, gi = (idx // ngrp, idx % ngrp) if ngrp > 1 else (idx, 0)

            def rows_of(r_d, blk, n):
                if dil <= NRES:
                    a = blk * sl
                    a = a if isinstance(a, int) else pl.multiple_of(a, 8)
                    return [(dil * c + r_d, pl.ds(a, n * sl)) for c in range(u)]
                return [(r_d % NRES, pl.ds(stride * BAND * blk + r_d // NRES, n * BAND, stride=stride))]

            def slabs(ref, r_d, blk, n):
                parts = [ref[i0, rows, :] for i0, rows in rows_of(r_d, blk, n)]
                return [jnp.concatenate([p[j * sl:(j + 1) * sl] for p in parts], axis=0) for j in range(n)]

            tasks = []
            for ri in range(nres):
                r_d = rr * nres + ri
                a0 = gi * grp
                kb = [x.astype(BF16) for x in slabs(k_ref.at[0], r_d, a0, grp)]
                vb = [x.astype(BF16) for x in slabs(v_ref.at[0], r_d, a0, grp)]
                if ngrp > 1:
                    ap = jnp.maximum(a0 - 1, 0)
                    kb.insert(0, slabs(k_ref.at[0], r_d, ap, 1)[0].astype(BF16))
                    vb.insert(0, slabs(v_ref.at[0], r_d, ap, 1)[0].astype(BF16))
                qs = slabs(q_ref.at[0], r_d, a0, grp)
                if not first:
                    olds = [slabs(ref, r_d, a0, grp) for ref in (m0_ref, m1_ref, l0_ref, l1_ref, acc_ref)]
                for j in range(grp):
                    jj = j + (1 if ngrp > 1 else 0)
                    old = None
                    if not first:
                        old = (jnp.concatenate([olds[0][j], olds[1][j]], axis=0),
                               jnp.concatenate([olds[2][j], olds[3][j]], axis=0), olds[4][j])
                    if jj == 0:
                        kk, vv, mask = kb[0], vb[0], tri
                    else:
                        kk = jnp.concatenate([kb[jj - 1], kb[jj]], axis=0)
                        vv = jnp.concatenate([vb[jj - 1], vb[jj]], axis=0)
                        mask = band & (in_cur | (gi > 0)) if (j == 0 and ngrp > 1) else band
                    tasks.append(((r_d, a0 + j), qs[j], kk, vv, mask, old))
            results = []
            for rows_j, q, kk, vv, mask, old in tasks:
                qq = jnp.concatenate([jnp.where(head0, q, 0.0), jnp.where(head0, 0.0, q)], axis=0).astype(BF16)
                s = jnp.where(mask, _dot_nt(qq, kk), NEG_INF)
                rm = jnp.max(s, axis=1, keepdims=True)
                if first:
                    mn = jnp.broadcast_to(rm, (2 * BAND, LANES))
                else:
                    mn = jnp.maximum(old[0], rm)
                p = jnp.exp(s - jnp.concatenate([mn] * (s.shape[1] // LANES), axis=1))
                pv = _dot(p.astype(BF16), jnp.concatenate([vv, jnp.ones_like(vv)], axis=1))
                ls = pv[:, LANES:]
                acc = jnp.where(head0, pv[:BAND, :LANES], pv[BAND:, :LANES])
                if first:
                    ln = ls
                else:
                    a = jnp.exp(old[0] - mn)
                    ln = a * old[1] + ls
                    acc = jnp.where(head0, a[:BAND], a[BAND:]) * old[2] + acc
                results.append((rows_j, mn, ln, acc))
            for (r_d, blk), mn, ln, acc in results:
                if last:
                    o_ref[0, pl.ds(dil * BAND * blk + r_d, BAND, stride=dil), :] = (
                        acc / jnp.where(head0, ln[:BAND], ln[BAND:]))
                else:
                    for ref, val in ((m0_ref, mn[:BAND]), (m1_ref, mn[BAND:]), (l0_ref, ln[:BAND]),
                                     (l1_ref, ln[BAND:]), (acc_ref, acc)):
                        for c, (i0, rows) in enumerate(rows_of(r_d, blk, 1)):
                            ref[i0, rows, :] = val[c * sl:(c + 1) * sl]
            return carry

        lax.fori_loop(0, (dil * nb) // ATTN_GROUP, group, 0)


def _attention(q, k, v):
    b, _, la, wa = q.shape
    s = la * NRES
    spec = pl.BlockSpec((1, NRES, la, LANES), lambda i, j: (i, 0, 0, j))
    scratch = [pltpu.VMEM((NRES, la, LANES), F32) for _ in range(5)]
    return pl.pallas_call(
        functools.partial(_attn_body, seq=s),
        grid=(b, wa // LANES),
        in_specs=[spec, spec, spec],
        out_specs=pl.BlockSpec((1, s, LANES), lambda i, j: (i, 0, j)),
        out_shape=jax.ShapeDtypeStruct((b, s, wa), F32),
        scratch_shapes=scratch,
        compiler_params=_cparams(2),
        name="dilated_attn",
    )(q, k, v)


def _mlstm_body(q_ref, k_ref, v_ref, g_ref, sq_ref, skn_ref, svn_ref, skt_ref, svt_ref,
                h_ref, c_ref, n_ref, m_ref, so_ref, cst_ref, nst_ref, mst_ref, gs_ref, *, nchunks, nheads):
    _sample_attn_rows(sq_ref, skn_ref, svn_ref, skt_ref, svt_ref, so_ref)
    L = CHUNK
    npairs = nheads // 2
    row = lax.broadcasted_iota(jnp.int32, (L, L), 0)
    lane = lax.broadcasted_iota(jnp.int32, (L, L), 1)
    lane_h = lax.broadcasted_iota(jnp.int32, (nheads, L), 1)
    causal_t = row <= lane
    head0_lane = lane < HEAD_DIM
    head0_row = row < HEAD_DIM
    blockdiag = head0_row == head0_lane
    lane8 = lax.broadcasted_iota(jnp.int32, (8, L), 1)
    row8 = lax.broadcasted_iota(jnp.int32, (8, L), 0)
    pad = jnp.zeros((L - 2 * nheads, L), F32)

    @pl.when(pl.program_id(1) == 0)
    def _():
        cst_ref[...] = jnp.zeros_like(cst_ref)
        nst_ref[...] = jnp.zeros_like(nst_ref)
        mst_ref[...] = jnp.zeros_like(mst_ref)

    def scan_lanes(x, op, fill):
        sh = 1
        while sh < L:
            x = op(x, jnp.where(lane_h >= sh, pltpu.roll(x, sh, 1), fill))
            sh *= 2
        return x

    m_prev = mst_ref[...]
    for c in range(nchunks):
        sl = slice(c * L, (c + 1) * L)
        b_c = scan_lanes(g_ref[0, nheads:2 * nheads, sl], jnp.add, 0.0)
        g_c = g_ref[0, 0:nheads, sl] - b_c
        cm_c = scan_lanes(g_c, jnp.maximum, NEG_INF)
        mm_c = jnp.maximum(cm_c, m_prev)
        mm_last = jnp.maximum(jnp.broadcast_to(cm_c[:, L - 1:L], (nheads, L)), m_prev)
        gs_ref[0, :, sl] = g_c
        gs_ref[1, :, sl] = mm_c
        gs_ref[2, :, sl] = jnp.exp(m_prev - mm_c)
        gs_ref[3, :, sl] = jnp.exp(-(b_c + mm_c))
        gs_ref[4, :, sl] = jnp.exp(g_c - mm_last)
        m_prev = jnp.broadcast_to(b_c[:, L - 1:L], (nheads, L)) + mm_last
    mst_ref[...] = m_prev

    def chunk(c, carry):
        t0 = pl.multiple_of(c * L, L)
        g_all = gs_ref[0, :, pl.ds(t0, L)]
        mm_all = gs_ref[1, :, pl.ds(t0, L)]
        iw_all = gs_ref[2, :, pl.ds(t0, L)]
        en_all = gs_ref[3, :, pl.ds(t0, L)]
        ws_all = gs_ref[4, :, pl.ds(t0, L)]
        cols = jnp.concatenate([g_all, ws_all, pad], axis=0).T
        for p in range(npairs):
            lanes = slice(p * LANES, (p + 1) * LANES)
            h0, h1 = 2 * p, 2 * p + 1
            q = q_ref[0, pl.ds(t0, L), lanes]
            k = k_ref[0, pl.ds(t0, L), lanes]
            v = v_ref[0, pl.ds(t0, L), lanes]
            zero = jnp.zeros_like(q)
            qq = jnp.concatenate([jnp.where(head0_lane, q, zero), jnp.where(head0_lane, zero, q)], axis=0)
            qk = _dot_nt(k, qq)
            cst = cst_ref[p]
            nst = nst_ref[p]
            sc, rs = [], []
            for h, hh in enumerate((h0, h1)):
                d = jnp.where(causal_t, cols[:, hh:hh + 1] - mm_all[hh:hh + 1, :], NEG_INF)
                s_h = qk[:, h * L:(h + 1) * L] * jnp.exp(d)
                sc.append(s_h.astype(BF16))
                rs.append(jnp.sum(s_h, axis=0, keepdims=True))
            ws_l = jnp.where(head0_lane, cols[:, nheads + h0:nheads + h0 + 1], cols[:, nheads + h1:nheads + h1 + 1])
            kws = k.astype(F32) * ws_l
            u = _dot_tn(v, jnp.concatenate(sc + [kws.astype(BF16)], axis=1))
            intra = jnp.where(head0_row, u[:, :L], u[:, L:2 * L])
            inter = _dot_nt(cst.astype(BF16), q)
            qn = _dot_nt(nst.astype(BF16), q)
            iw_l = jnp.where(head0_row, iw_all[h0:h0 + 1, :], iw_all[h1:h1 + 1, :])
            den0 = iw_all[h0:h0 + 1, :] * qn[0:1, :] + rs[0]
            den1 = iw_all[h1:h1 + 1, :] * qn[1:2, :] + rs[1]
            lim = jnp.maximum(jnp.abs(jnp.where(head0_row, den0, den1)),
                              jnp.where(head0_row, en_all[h0:h0 + 1, :], en_all[h1:h1 + 1, :]))
            h_ref[0, pl.ds(t0, L), lanes] = ((iw_l * inter + intra) / lim).T

            cw0 = iw_all[h0:h0 + 1, L - 1:L]
            cw1 = iw_all[h1:h1 + 1, L - 1:L]
            cst_ref[p] = jnp.where(head0_row, cw0, cw1) * cst + jnp.where(blockdiag, u[:, 2 * L:], 0.0)
            ksum = jnp.sum(kws, axis=0, keepdims=True)
            own = (row8 == 0) == (lane8 < HEAD_DIM)
            nst_ref[p] = jnp.where((row8 < 2) & own, jnp.where(row8 == 0, cw0, cw1) * nst + ksum, 0.0)
        return carry

    lax.fori_loop(0, nchunks, chunk, 0, unroll=2)

    @pl.when(pl.program_id(1) == pl.num_programs(1) - 1)
    def _():
        for p in range(npairs):
            for h in range(2):
                sl = slice(h * HEAD_DIM, (h + 1) * HEAD_DIM)
                c_ref[0, 2 * p + h] = cst_ref[p, sl, sl]
                n_ref[0, 2 * p + h:2 * p + h + 1, :] = nst_ref[p, h:h + 1, sl]
        m_ref[0] = mst_ref[...]


def _mlstm(q, k, v, gates, sq, skn, svn, ckt, cvt, layer, *, ts):
    b, s, wb = q.shape
    nheads = wb // HEAD_DIM
    nj = s // ts
    dec_b, nh_a, dh = sq.shape
    per_step = dec_b // (b * nj)
    assert per_step * b * nj == dec_b
    spec = pl.BlockSpec((1, ts, wb), lambda i, j: (i, j, 0))
    srow = pl.BlockSpec((per_step, nh_a, dh), lambda i, j: (i * nj + j, 0, 0))
    cache = pl.BlockSpec((None, per_step, nh_a, dh, ckt.shape[-1]), lambda i, j: (layer, i * nj + j, 0, 0, 0))
    out_shape = (jax.ShapeDtypeStruct((b, s, wb), F32),
                 jax.ShapeDtypeStruct((b, nheads, HEAD_DIM, HEAD_DIM), F32),
                 jax.ShapeDtypeStruct((b, nheads, HEAD_DIM), F32),
                 jax.ShapeDtypeStruct((b, nheads, LANES), F32),
                 jax.ShapeDtypeStruct((dec_b, nh_a, dh), F32))
    out_specs = (spec,
                 pl.BlockSpec((1, nheads, HEAD_DIM, HEAD_DIM), lambda i, j: (i, 0, 0, 0)),
                 pl.BlockSpec((1, nheads, HEAD_DIM), lambda i, j: (i, 0, 0)),
                 pl.BlockSpec((1, nheads, LANES), lambda i, j: (i, 0, 0)),
                 srow)
    return pl.pallas_call(
        functools.partial(_mlstm_body, nchunks=ts // CHUNK, nheads=nheads),
        grid=(b, nj),
        in_specs=[spec, spec, spec, pl.BlockSpec((1, 2 * nheads, ts), lambda i, j: (i, 0, j)),
                  srow, srow, srow, cache, cache],
        out_specs=out_specs,
        out_shape=out_shape,
        scratch_shapes=[pltpu.VMEM((nheads // 2, LANES, LANES), F32), pltpu.VMEM((nheads // 2, 8, LANES), F32),
                        pltpu.VMEM((nheads, LANES), F32), pltpu.VMEM((5, nheads, ts), F32)],
        compiler_params=_cparams(2),
        name="mlstm",
    )(q, k, v, gates, sq, skn, svn, ckt, cvt)


def _post_body(*refs, prompt, seq_tiles, ck):
    (x_ref, oa_ref, hb_ref, og_ref, p_ref, ga_ref, gb_ref, wo_ref, g_ref, wup_ref, cw_ref, cb_ref, wdn_ref,
     gp_ref, wpg_ref, wpp_ref) = refs[:16]
    if prompt:
        o_ref, conv_ref, y_ref, carry_ref = refs[16:]
    else:
        p0_ref, p1_ref, o_ref, conv_ref, y_ref = refs[16:]
    tm = x_ref.shape[0]
    dff = wdn_ref.shape[0]
    wa = oa_ref.shape[1]
    ya = _rms(oa_ref[...], ga_ref[...]).astype(BF16)
    yb = (og_ref[...] * _rms(hb_ref[...], gb_ref[...])).astype(BF16)
    x = x_ref[...] + _dot(ya, wo_ref[0:wa, :]) + _dot(yb, wo_ref[wa:, :])
    h = _rms(x, g_ref[...]).astype(BF16)
    if prompt:
        @pl.when(pl.program_id(0) % seq_tiles == 0)
        def _():
            carry_ref[...] = jnp.zeros_like(carry_ref)
        row = lax.broadcasted_iota(jnp.int32, (tm, ck), 0)
    for c in range(dff // ck):
        cs = slice(c * ck, (c + 1) * ck)
        gate = _dot(h, wup_ref[:, c * ck:(c + 1) * ck])
        val = _dot(h, wup_ref[:, dff + c * ck:dff + (c + 1) * ck])
        if prompt:
            b0 = carry_ref[0:1, cs]
            b1 = carry_ref[1:2, cs]
            g1 = jnp.where(row == 0, b1, pltpu.roll(gate, 1, 0))
            g2 = jnp.where(row == 0, b0, jnp.where(row == 1, b1, pltpu.roll(gate, 2, 0)))
            carry_ref[:, cs] = gate[tm - 2:tm, :]
        else:
            g2 = p0_ref[:, cs]
            g1 = p1_ref[:, cs]
            conv_ref[:, cs] = gate
        gc = cb_ref[:, cs] + (cw_ref[0:1, cs] * g2 + cw_ref[1:2, cs] * g1 + cw_ref[2:3, cs] * gate)
        y_ref[:, cs] = (_gelu_tanh(gc) * val).astype(BF16)
    if prompt:
        conv_ref[0] = carry_ref[...]
    x = x + _dot(y_ref[...], wdn_ref[...])
    gate = jax.nn.sigmoid(_dot(_rms(x, gp_ref[...]).astype(BF16), wpg_ref[...]))
    o_ref[...] = x + gate * _dot(p_ref[...].astype(BF16), wpp_ref[...])


def _post(x, oa, hb, og, p, layer, params, prev=None, *, tm, seq_tiles=1, batch=1):
    t, d = x.shape
    dff = params[7].shape[1]
    prompt = prev is None
    tok = lambda w_: pl.BlockSpec((tm, w_), lambda i: (i, 0))
    in_specs = [tok(d), tok(oa.shape[1]), tok(hb.shape[1]), tok(og.shape[1]),
                pl.BlockSpec((None, tm, p.shape[2]), lambda i: (layer, i, 0))]
    in_specs += [_layer_spec(a, layer) for a in params]
    args = [x, oa, hb, og, p, *params]
    scratch = [pltpu.VMEM((tm, dff), BF16)]
    if prompt:
        conv_shape = jax.ShapeDtypeStruct((batch, 2, dff), F32)
        conv_spec = pl.BlockSpec((1, 2, dff), lambda i: (i // seq_tiles, 0, 0))
        scratch.append(pltpu.VMEM((2, dff), F32))
    else:
        in_specs += [tok(dff), tok(dff)]
        args += list(prev)
        conv_shape = jax.ShapeDtypeStruct((t, dff), F32)
        conv_spec = tok(dff)
    return pl.pallas_call(
        functools.partial(_post_body, prompt=prompt, seq_tiles=seq_tiles, ck=FFN_CHUNK),
        grid=(t // tm,),
        in_specs=in_specs,
        out_specs=(tok(d), conv_spec),
        out_shape=(jax.ShapeDtypeStruct((t, d), F32), conv_shape),
        scratch_shapes=scratch,
        compiler_params=_cparams(1),
        name="post_mixer",
    )(*args)


def _window_body(k_ref, v_ref, kacc_ref, vacc_ref, ko_ref, vo_ref):
    del kacc_ref, vacc_ref
    ko_ref[...] = k_ref[0].T
    vo_ref[...] = v_ref[0].T


def _window_rows(k, v, kacc, vacc, layer, *, keep, tpos):
    b, s, w = k.shape
    first = (s - keep) // tpos
    src = pl.BlockSpec((1, tpos, w), lambda i, j: (i, first + j, 0))
    acc = pl.BlockSpec(memory_space=pl.ANY)
    dst = pl.BlockSpec((None, None, w, tpos), lambda i, j: (layer, i, 0, j))
    return pl.pallas_call(
        _window_body,
        grid=(b, keep // tpos),
        in_specs=[src, src, acc, acc],
        out_specs=(dst, dst),
        out_shape=(jax.ShapeDtypeStruct(kacc.shape, F32), jax.ShapeDtypeStruct(vacc.shape, F32)),
        input_output_aliases={2: 0, 3: 1},
        compiler_params=_cparams(2),
        name="window_rows",
    )(k, v, kacc, vacc)


def _sample_attn_rows(q_ref, kn_ref, vn_ref, kt_ref, vt_ref, o_ref):
    nrows, nh, _, wbuf = kt_ref.shape
    r = lax.broadcasted_iota(jnp.int32, (HEAD_DIM, HEAD_DIM), 0)
    c = lax.broadcasted_iota(jnp.int32, (HEAD_DIM, HEAD_DIM), 1)
    diag = r == c
    dist = wbuf - lax.broadcasted_iota(jnp.int32, (1, wbuf), 1)
    for b in range(nrows):
        q = q_ref[b]
        kn = kn_ref[b]
        vn = vn_ref[b]
        s_new = jnp.sum(q * kn, axis=-1, keepdims=True)
        rows = []
        for h in range(nh):
            q_col = jnp.sum(jnp.where(diag, q[h:h + 1, :], 0.0), axis=1, keepdims=True)
            rows.append(jnp.sum(kt_ref[b, h] * q_col, axis=0, keepdims=True))
        s = jnp.concatenate(rows, axis=0)
        stats = []
        for dil in DILATIONS:
            mask = ((dist & (dil - 1)) == 0) & (dist <= BAND * dil)
            sg = jnp.where(mask, s, NEG_INF)
            m = jnp.maximum(jnp.max(sg, axis=1, keepdims=True), s_new)
            e = jnp.exp(sg - m)
            e_new = jnp.exp(s_new - m)
            stats.append((m, e, e_new, jnp.sum(e, axis=1, keepdims=True) + e_new))
        mm = jnp.maximum(jnp.maximum(stats[0][0], stats[1][0]), stats[2][0])
        cs = [jnp.exp(m - mm) for m, _, _, _ in stats]
        wgt = cs[0] * stats[0][1] + cs[1] * stats[1][1] + cs[2] * stats[2][1]
        w_new = cs[0] * stats[0][2] + cs[1] * stats[1][2] + cs[2] * stats[2][2]
        den = cs[0] * stats[0][3] + cs[1] * stats[1][3] + cs[2] * stats[2][3]
        outs = []
        for h in range(nh):
            o_col = jnp.sum(vt_ref[b, h] * wgt[h:h + 1, :], axis=1, keepdims=True)
            outs.append(jnp.sum(jnp.where(diag, o_col, 0.0), axis=0, keepdims=True))
        o_ref[b] = (jnp.concatenate(outs, axis=0) + w_new * vn) / den


def _smlstm_body(q_ref, k_ref, v_ref, ig_ref, lf_ref, c_ref, n_ref, m_ref, h_ref, co_ref, no_ref, mo_ref):
    q, k, v = q_ref[0], k_ref[0], v_ref[0]
    ig, lf, m = ig_ref[0], lf_ref[0], m_ref[0]
    n = n_ref[0]
    nh = q.shape[0]
    inter = lf + m
    m_t = jnp.maximum(inter, ig)
    dw = jnp.exp(ig - m_t)
    iw = jnp.exp(inter - m_t)
    sc = jnp.sum(q * k, axis=-1, keepdims=True) * dw
    r = lax.broadcasted_iota(jnp.int32, (HEAD_DIM, HEAD_DIM), 0)
    l = lax.broadcasted_iota(jnp.int32, (HEAD_DIM, HEAD_DIM), 1)
    cq = []
    for h in range(nh):
        c_h = c_ref[0, h]
        q_h = jnp.broadcast_to(q[h:h + 1, :], (8, HEAD_DIM)).astype(BF16)
        cq.append(_dot_nt(q_h, c_h.astype(BF16))[0:1, :])
        v_col = jnp.sum(jnp.where(r == l, v[h:h + 1, :], 0.0), axis=1, keepdims=True)
        co_ref[0, h] = iw[h:h + 1, :] * c_h + (dw[h:h + 1, :] * v_col) * k[h:h + 1, :]
    num = iw * jnp.concatenate(cq, axis=0) + sc * v
    den = iw * jnp.sum(n * q, axis=-1, keepdims=True) + sc
    h_ref[0] = num / jnp.maximum(jnp.abs(den), jnp.exp(-m_t))
    no_ref[0] = iw * n + dw * k
    mo_ref[0] = m_t


def _sample_mlstm(q, k, v, ig, lf, c, n, m):
    b, nh, dh = q.shape
    row = pl.BlockSpec((1, nh, dh), lambda i: (i, 0, 0))
    col = pl.BlockSpec((1, nh, 1), lambda i: (i, 0, 0))
    cst = pl.BlockSpec((1, nh, dh, dh), lambda i: (i, 0, 0, 0))
    return pl.pallas_call(
        _smlstm_body,
        grid=(b,),
        in_specs=[row, row, row, col, col, cst, row, col],
        out_specs=(row, cst, row, col),
        out_shape=(jax.ShapeDtypeStruct((b, nh, dh), F32), jax.ShapeDtypeStruct((b, nh, dh, dh), F32),
                   jax.ShapeDtypeStruct((b, nh, dh), F32), jax.ShapeDtypeStruct((b, nh, 1), F32)),
        compiler_params=_cparams(1),
        name="sample_mlstm",
    )(q, k, v, ig, lf, c, n, m)


def _rope_tables(pos):
    half = ROT_DIM // 2
    inv = ROPE_THETA ** (-jnp.arange(half, dtype=F32) / half)
    ang = pos.astype(F32)[:, None] * inv[None, :]
    cos, sin = jnp.cos(ang), jnp.sin(ang)
    n = pos.shape[0]
    rest = HEAD_DIM - ROT_DIM
    c = jnp.concatenate([cos, cos, jnp.ones((n, rest), F32)], axis=1)
    sa = jnp.concatenate([-sin, jnp.zeros((n, half + rest), F32)], axis=1)
    sb = jnp.concatenate([jnp.zeros((n, half), F32), sin, jnp.zeros((n, rest), F32)], axis=1)
    rep = LANES // HEAD_DIM
    return tuple(jnp.concatenate([t] * rep, axis=1) for t in (c, sa, sb))


def _rope_angles_t(pos):
    half = ROT_DIM // 2
    inv = ROPE_THETA ** (-jnp.arange(half, dtype=F32) / half)
    ang = pos.astype(F32)[:, None] * inv[None, :]
    return jnp.cos(ang).T, jnp.sin(ang).T


def kernel(x_prompt, x_sample, cache_win_k, cache_win_v, state_C, state_n, state_m, state_conv, p_prompt, p_sample,
           norm_mix, w_in, q_norm, k_norm, b_gates, out_norm_a, out_norm_b, w_out, norm_ffn, w_up, conv_w, conv_b,
           w_down, norm_ple, w_ple_gate, w_ple_proj):
    batch, seq, d = x_prompt.shape
    dec_batch, dec_seq, _ = x_sample.shape
    depth = w_in.shape[0]
    nh_a = cache_win_k.shape[3]
    nh_b = state_C.shape[2]
    wa, wb = nh_a * HEAD_DIM, nh_b * HEAD_DIM
    wbuf = cache_win_k.shape[2]
    win_max = BAND * DILATIONS[-1]
    keep = min(win_max, seq)
    assert dec_seq == 1 and wbuf == win_max and seq % win_max == 0
    assert w_in.shape[2] == 3 * wa + 4 * wb + 2 * nh_b and wa % LANES == 0 and wb % LANES == 0

    tm = min(TOKEN_TILE, seq)
    seq_tiles = seq // tm
    tp, ts = batch * seq, dec_batch * dec_seq

    cut = 3 * wa + 4 * wb
    w_main = w_in[:, :, :cut].astype(BF16)
    w_gate = w_in[:, :, cut + 2 * nh_b - LANES:].astype(BF16)
    bg = jnp.concatenate([jnp.zeros((depth, GATE_LANE0), F32), b_gates], axis=1)[:, None, :]
    qn = jnp.tile(q_norm, (1, nh_a))[:, None, :]
    kn = jnp.tile(k_norm, (1, nh_a))[:, None, :]
    rows = lambda a: a[:, None, :]
    cols = lambda a: jnp.broadcast_to(a[:, 0, :, None], (depth, wa, LANES))
    inproj_params_s = (rows(norm_mix), w_main, w_gate, qn, kn, bg)
    inproj_params_p = (rows(norm_mix), w_main, w_gate, cols(qn), cols(kn), bg)
    post_params = (rows(out_norm_a), rows(out_norm_b), w_out.astype(BF16), rows(norm_ffn), w_up.astype(BF16),
                   conv_w, rows(conv_b), w_down.astype(BF16), rows(norm_ple), w_ple_gate.astype(BF16),
                   w_ple_proj.astype(BF16))
    pp = p_prompt.reshape(depth, tp, -1)
    ps = p_sample.reshape(depth, ts, -1)
    head_of_lane = jnp.arange(wa) // HEAD_DIM
    ind = (head_of_lane[:, None] == jnp.arange(LANES)[None, :]).astype(BF16)
    indt = ind.T
    tab_p = _rope_angles_t(jnp.arange(seq, dtype=jnp.int32))
    tab_s = tuple(jnp.broadcast_to(t, (ts, LANES))
                  for t in _rope_tables(PAST_LEN + jnp.arange(dec_seq, dtype=jnp.int32)))

    ckt = jnp.transpose(cache_win_k, (0, 1, 3, 4, 2))
    cvt = jnp.transpose(cache_win_v, (0, 1, 3, 4, 2))

    xp = x_prompt.reshape(tp, d)
    xs = x_sample.reshape(ts, d)
    outs = [[] for _ in range(12)]
    wk_acc = jnp.zeros((depth, batch, wa, keep), F32)
    wv_acc = jnp.zeros_like(wk_acc)
    for i in range(depth):
        q_rm, kp, vp, qbp, kbp, vbp, ogp, gt, k_rm, v_rm = _inproj(
            xp, i, *inproj_params_p, tab_p, (),
            tm=tm, seq_tiles=seq_tiles, batch=batch, mdtype=BF16, transpose_gates=True)
        qa, ka, va, qb, kb, vb, og, ga = _inproj(
            xs, i, *inproj_params_s, tab_s, (ind, indt),
            tm=ts, seq_tiles=1, batch=dec_batch, mdtype=F32, transpose_gates=False)
        r3 = lambda a: a.reshape(batch, seq, a.shape[-1])
        hd = lambda a: a.reshape(dec_batch, -1, HEAD_DIM)
        oap = _attention(q_rm, k_rm, v_rm)
        hbp, c_new, n_new, m_new, oa = _mlstm(r3(qbp), r3(kbp), r3(vbp), gt, hd(qa), hd(ka), hd(va), ckt, cvt, i,
                                              ts=min(seq, MLSTM_TILE))
        xp, conv_new = _post(xp, oap.reshape(tp, wa), hbp.reshape(tp, wb), ogp, pp, i, post_params,
                             tm=tm, seq_tiles=seq_tiles, batch=batch)
        wk_acc, wv_acc = _window_rows(r3(kp), r3(vp), wk_acc, wv_acc, i, keep=keep, tpos=tm)
        outs[2].append(c_new)
        outs[3].append(n_new)
        outs[4].append(m_new[..., 0])
        outs[5].append(conv_new)
        ig = ga[:, GATE_LANE0:GATE_LANE0 + nh_b, None]
        lf = ga[:, GATE_LANE0 + nh_b:, None]
        hb, c_new, n_new, m_new = _sample_mlstm(hd(qb), hd(kb), hd(vb), ig, lf, state_C[i], state_n[i],
                                                state_m[i][..., None])
        xs, gate_new = _post(xs, oa.reshape(ts, wa), hb.reshape(ts, wb), og, ps, i, post_params,
                             prev=(state_conv[i][:, 0], state_conv[i][:, 1]), tm=ts)
        outs[6].append(ka.reshape(dec_batch, dec_seq, nh_a, HEAD_DIM))
        outs[7].append(va.reshape(dec_batch, dec_seq, nh_a, HEAD_DIM))
        outs[8].append(c_new)
        outs[9].append(n_new)
        outs[10].append(m_new[..., 0])
        outs[11].append(jnp.stack([state_conv[i][:, 1], gate_new], axis=1))
    def window_out(acc):
        return jnp.transpose(acc.reshape(depth, batch, nh_a, HEAD_DIM, keep), (0, 1, 4, 2, 3))

    st = jnp.stack
    return ((xp.reshape(batch, seq, d), xs.reshape(dec_batch, dec_seq, d), window_out(wk_acc), window_out(wv_acc))
            + tuple(st(o) for o in outs[2:]))
```

```python
import functools

import jax
import jax.numpy as jnp
from jax import lax
from jax.experimental import pallas as pl
from jax.experimental.pallas import tpu as pltpu

F32 = jnp.float32
BF16 = jnp.bfloat16

HEAD_DIM = 64
ROT_DIM = HEAD_DIM // 4
ROPE_THETA = 500000.0
RMS_EPS = 1e-6
NEG_INF = -1e30
PAST_LEN = 8192
DILATIONS = (1, 4, 16)
NRES = 4
BAND = 128
ATTN_GROUP = 8
CHUNK = 128
FFN_CHUNK = 256
TOKEN_TILE = 512
MLSTM_TILE = 1024
SAMPLE_ROWS = 8
LANES = 128
GATE_LANE0 = LANES - 16
VMEM_LIMIT = 56 * 1024 * 1024


def _cparams(n_axes):
    return pltpu.CompilerParams(dimension_semantics=("arbitrary",) * n_axes,
                                vmem_limit_bytes=VMEM_LIMIT)


def _const_spec(shape):
    return pl.BlockSpec(shape, lambda *_: (0,) * len(shape))


def _layer_spec(stack, layer):
    nd = stack.ndim
    return pl.BlockSpec((None,) + stack.shape[1:], lambda *_: (layer,) + (0,) * (nd - 1),
                        pipeline_mode=pl.Buffered(1))


def _rms(x, g):
    return (x * lax.rsqrt(jnp.mean(x * x, axis=-1, keepdims=True) + RMS_EPS)) * g


def _dot(a, b):
    return jnp.dot(a, b, preferred_element_type=F32)


def _dot_nt(a, b):
    return lax.dot_general(a, b, (((1,), (1,)), ((), ())), preferred_element_type=F32)


def _dot_tn(a, b):
    return lax.dot_general(a, b, (((0,), (0,)), ((), ())), preferred_element_type=F32)


def _split_dot(a, b):
    hi = a.astype(BF16)
    lo = (a - hi.astype(F32)).astype(BF16)
    return _dot(hi, b) + _dot(lo, b)


def _log_sigmoid(x):
    return jnp.minimum(x, 0.0) - jnp.log1p(jnp.exp(-jnp.abs(x)))


def _gelu_tanh(x):
    c = 0.7978845608028654
    return x * (0.5 * (1.0 + jnp.tanh(c * (x + 0.044715 * (x * x * x)))))


def _inproj_body(x_ref, g_ref, w_ref, wg_ref, qn_ref, kn_ref, bg_ref, *refs, wa, transpose_gates, n_acc=0):
    h = _rms(x_ref[...], g_ref[...]).astype(BF16)
    proj = lambda lo, hi: _dot_nt(h, w_ref[lo:hi, :])
    reps = wa // LANES
    half = ROT_DIM // 2
    if transpose_gates:
        cos_ref, sin_ref = refs[:2]
        qa_ref, ka_ref, va_ref, qb_ref, kb_ref, vb_ref, og_ref, gt_ref, *rest = refs[2 + n_acc:]
        tiles = x_ref.shape[0] // LANES

        def head_norm_rope(z, gain_ref):
            zt = z.T
            cos, sin = cos_ref[...], sin_ref[...]
            heads = []
            for hh in range(wa // HEAD_DIM):
                x = zt[hh * HEAD_DIM:(hh + 1) * HEAD_DIM, :]
                gain = jnp.concatenate([gain_ref[hh * HEAD_DIM:(hh + 1) * HEAD_DIM, :]] * tiles, axis=1)
                xn = (x * lax.rsqrt(jnp.mean(x * x, axis=0, keepdims=True) + RMS_EPS)) * gain
                x1, x2 = xn[0:half], xn[half:2 * half]
                heads.append(jnp.concatenate([x1 * cos - x2 * sin, x2 * cos + x1 * sin, xn[2 * half:]], axis=0))
            zt = jnp.concatenate(heads, axis=0)
            return zt, zt.T
    else:
        cos_ref, sa_ref, sb_ref, ind_ref, indt_ref = refs[:5]
        qa_ref, ka_ref, va_ref, qb_ref, kb_ref, vb_ref, og_ref, gt_ref, *rest = refs[5:]
        cos = jnp.concatenate([cos_ref[...]] * reps, axis=1)
        sa = jnp.concatenate([sa_ref[...]] * reps, axis=1)
        sb = jnp.concatenate([sb_ref[...]] * reps, axis=1)
        ind = ind_ref[...]
        indt = indt_ref[...]

        def head_norm_rope(z, gain_ref):
            ss = _split_dot(z * z, ind)
            inv = lax.rsqrt(ss * (1.0 / HEAD_DIM) + RMS_EPS)
            zn = (z * _split_dot(inv, indt)) * gain_ref[...]
            return None, zn * cos + pltpu.roll(zn, wa - half, 1) * sa + pltpu.roll(zn, half, 1) * sb

    qa = head_norm_rope(proj(0, wa), qn_ref)[1] * (HEAD_DIM ** -0.5)
    kt, ka = head_norm_rope(proj(wa, 2 * wa), kn_ref)
    va = proj(2 * wa, 3 * wa)
    if transpose_gates:
        ka_ref[...] = kt
        va_ref[...] = va.T
        krm_ref, vrm_ref, zs_ref = rest
        rows = zs_ref.shape[1] // NRES
        for val, dst in ((qa, qa_ref), (ka, krm_ref), (va, vrm_ref)):
            for c in range(reps):
                zs_ref[c] = val[:, c * LANES:(c + 1) * LANES]
            for r in range(NRES):
                for c in range(reps):
                    dst[0, r, :, c * LANES:(c + 1) * LANES] = zs_ref[c, pl.ds(r, rows, stride=NRES), :]
    else:
        qa_ref[...] = qa
        ka_ref[...] = ka
        va_ref[...] = va
    wb = (w_ref.shape[0] - 3 * wa) // 4
    o = 3 * wa
    qb_ref[...] = proj(o, o + wb).astype(qb_ref.dtype)
    kb_ref[...] = (proj(o + wb, o + 2 * wb) * (HEAD_DIM ** -0.5)).astype(kb_ref.dtype)
    vb_ref[...] = proj(o + 2 * wb, o + 3 * wb).astype(vb_ref.dtype)
    og_ref[...] = jax.nn.sigmoid(proj(o + 3 * wb, o + 4 * wb))

    v = _dot_nt(h, wg_ref[...]) + bg_ref[...]
    lane = lax.broadcasted_iota(jnp.int32, v.shape, 1)
    ga = jnp.where(lane >= GATE_LANE0 + 8, _log_sigmoid(v), v)
    if transpose_gates:
        gt_ref[0] = ga.T[GATE_LANE0:, :]
    else:
        gt_ref[...] = ga


def _inproj(x, layer, gain, w_main, w_gate, qn, kn, bg, tables, consts, *, tm, seq_tiles, batch, mdtype,
            transpose_gates, window=None):
    t, d = x.shape
    wa = qn.shape[1] if transpose_gates else qn.shape[2]
    wb = (w_main.shape[1] - 3 * wa) // 4
    nt = t // tm
    tok = lambda w: pl.BlockSpec((tm, w), lambda i: (i, 0))
    sds = jax.ShapeDtypeStruct
    args = [x, gain, w_main, w_gate, qn, kn, bg, *tables, *consts]
    in_specs = [tok(d)] + [_layer_spec(a, layer) for a in (gain, w_main, w_gate, qn, kn, bg)]
    tail_shape = [sds((t, wb), mdtype), sds((t, wb), mdtype), sds((t, wb), mdtype), sds((t, wb), F32)]
    tail_specs = [tok(wb), tok(wb), tok(wb), tok(wb)]
    kwargs = {}
    if transpose_gates:
        seq = seq_tiles * tm
        keep, depth, k_acc, v_acc = window
        first = (seq - keep) // tm
        in_specs += [pl.BlockSpec((tables[0].shape[0], tm), lambda i: (0, i % seq_tiles))] * len(tables)
        rm_shape = sds((batch, NRES, seq // NRES, wa), F32)
        rm_spec = pl.BlockSpec((1, NRES, tm // NRES, wa), lambda i: (i // seq_tiles, 0, i % seq_tiles, 0))
        win_shape = sds((depth, batch, wa, keep), F32)
        win_spec = pl.BlockSpec((None, None, wa, tm),
                                lambda i: (layer, i // seq_tiles, 0, jnp.maximum(i % seq_tiles - first, 0)))
        gt_shape = sds((batch, 16, seq), F32)
        gt_spec = pl.BlockSpec((1, 16, tm), lambda i: (i // seq_tiles, 0, i % seq_tiles))
        out_shape = [rm_shape, win_shape, win_shape] + tail_shape + [gt_shape, rm_shape, rm_shape]
        out_specs = [rm_spec, win_spec, win_spec] + tail_specs + [gt_spec, rm_spec, rm_spec]
        scratch = [pltpu.VMEM((wa // LANES, tm, LANES), F32)]
        n_acc = 0
        if k_acc is not None:
            n_acc = 2
            kwargs["input_output_aliases"] = {len(args): 1, len(args) + 1: 2}
            args += [k_acc, v_acc]
            in_specs += [pl.BlockSpec(memory_space=pl.ANY)] * 2
        body = functools.partial(_inproj_body, wa=wa, transpose_gates=True, n_acc=n_acc)
    else:
        in_specs += [pl.BlockSpec((tm, LANES), lambda i: (i % seq_tiles, 0))] * len(tables)
        in_specs += [_const_spec(c.shape) for c in consts]
        out_shape = [sds((t, wa), F32)] * 3 + tail_shape + [sds((t, LANES), F32)]
        out_specs = [tok(wa)] * 3 + tail_specs + [tok(LANES)]
        scratch = []
        body = functools.partial(_inproj_body, wa=wa, transpose_gates=False)
    return pl.pallas_call(
        body,
        grid=(nt,),
        in_specs=in_specs,
        out_specs=tuple(out_specs),
        out_shape=tuple(out_shape),
        scratch_shapes=scratch,
        compiler_params=_cparams(1),
        name="inproj",
        **kwargs,
    )(*args)


def _attn_body(q_ref, k_ref, v_ref, o_ref, acc_ref, m0_ref, m1_ref, l0_ref, l1_ref, *, seq):
    head0 = lax.broadcasted_iota(jnp.int32, (BAND, LANES), 1) < HEAD_DIM
    qrow = lax.broadcasted_iota(jnp.int32, (2 * BAND, 2 * BAND), 0) % BAND
    kcol = lax.broadcasted_iota(jnp.int32, (2 * BAND, 2 * BAND), 1)
    in_cur = kcol >= BAND
    qrow1 = lax.broadcasted_iota(jnp.int32, (2 * BAND, BAND), 0) % BAND
    kcol1 = lax.broadcasted_iota(jnp.int32, (2 * BAND, BAND), 1)

    for pi, dil in enumerate(DILATIONS):
        nb = seq // (dil * BAND)
        first, last = pi == 0, pi == len(DILATIONS) - 1
        grp = min(ATTN_GROUP, nb)
        nres = ATTN_GROUP // grp
        ngrp = nb // grp
        u = max(NRES // dil, 1)
        sl = BAND // u
        stride = max(dil // NRES, 1)
        assert not last or u == 1
        pos = lambda i, u=u, sl=sl: u * (i % sl) + i // sl
        dist = (BAND + pos(qrow)) - (BAND * (kcol // BAND) + pos(kcol % BAND))
        band = (dist >= 0) & (dist <= BAND)
        tri = pos(kcol1) <= pos(qrow1)

        def group(idx, carry, dil=dil, first=first, last=last, grp=grp, nres=nres, ngrp=ngrp, u=u, sl=sl,
                  stride=stride, band=band, tri=tri):
            rr, gi = (idx // ngrp, idx % ngrp) if ngrp > 1 else (idx, 0)

            def rows_of(r_d, blk, n):
                if dil <= NRES:
                    a = blk * sl
                    a = a if isinstance(a, int) else pl.multiple_of(a, 8)
                    return [(dil * c + r_d, pl.ds(a, n * sl)) for c in range(u)]
                return [(r_d % NRES, pl.ds(stride * BAND * blk + r_d // NRES, n * BAND, stride=stride))]

            def slabs(ref, r_d, blk, n):
                parts = [ref[i0, rows, :] for i0, rows in rows_of(r_d, blk, n)]
                return [jnp.concatenate([p[j * sl:(j + 1) * sl] for p in parts], axis=0) for j in range(n)]

            tasks = []
            for ri in range(nres):
                r_d = rr * nres + ri
                a0 = gi * grp
                kb = [x.astype(BF16) for x in slabs(k_ref.at[0], r_d, a0, grp)]
                vb = [x.astype(BF16) for x in slabs(v_ref.at[0], r_d, a0, grp)]
                if ngrp > 1:
                    ap = jnp.maximum(a0 - 1, 0)
                    kb.insert(0, slabs(k_ref.at[0], r_d, ap, 1)[0].astype(BF16))
                    vb.insert(0, slabs(v_ref.at[0], r_d, ap, 1)[0].astype(BF16))
                qs = slabs(q_ref.at[0], r_d, a0, grp)
                if not first:
                    olds = [slabs(ref, r_d, a0, grp) for ref in (m0_ref, m1_ref, l0_ref, l1_ref, acc_ref)]
                for j in range(grp):
                    jj = j + (1 if ngrp > 1 else 0)
                    old = None
                    if not first:
                        old = (jnp.concatenate([olds[0][j], olds[1][j]], axis=0),
                               jnp.concatenate([olds[2][j], olds[3][j]], axis=0), olds[4][j])
                    if jj == 0:
                        kk, vv, mask = kb[0], vb[0], tri
                    else:
                        kk = jnp.concatenate([kb[jj - 1], kb[jj]], axis=0)
                        vv = jnp.concatenate([vb[jj - 1], vb[jj]], axis=0)
                        mask = band & (in_cur | (gi > 0)) if (j == 0 and ngrp > 1) else band
                    tasks.append(((r_d, a0 + j), qs[j], kk, vv, mask, old))
            results = []
            for rows_j, q, kk, vv, mask, old in tasks:
                qq = jnp.concatenate([jnp.where(head0, q, 0.0), jnp.where(head0, 0.0, q)], axis=0).astype(BF16)
                s = jnp.where(mask, _dot_nt(qq, kk), NEG_INF)
                rm = jnp.max(s, axis=1, keepdims=True)
                if first:
                    mn = jnp.broadcast_to(rm, (2 * BAND, LANES))
                else:
                    mn = jnp.maximum(old[0], rm)
                p = jnp.exp((s - jnp.concatenate([mn] * (s.shape[1] // LANES), axis=1)).astype(BF16))
                pv = _dot(p, jnp.concatenate([vv, jnp.ones_like(vv)], axis=1))
                ls = pv[:, LANES:]
                acc = jnp.where(head0, pv[:BAND, :LANES], pv[BAND:, :LANES])
                if first:
                    ln = ls
                else:
                    a = jnp.exp(old[0] - mn)
                    ln = a * old[1] + ls
                    acc = jnp.where(head0, a[:BAND], a[BAND:]) * old[2] + acc
                results.append((rows_j, mn, ln, acc))
            for (r_d, blk), mn, ln, acc in results:
                if last:
                    o_ref[0, pl.ds(dil * BAND * blk + r_d, BAND, stride=dil), :] = (
                        acc / jnp.where(head0, ln[:BAND], ln[BAND:]))
                else:
                    for ref, val in ((m0_ref, mn[:BAND]), (m1_ref, mn[BAND:]), (l0_ref, ln[:BAND]),
                                     (l1_ref, ln[BAND:]), (acc_ref, acc)):
                        for c, (i0, rows) in enumerate(rows_of(r_d, blk, 1)):
                            ref[i0, rows, :] = val[c * sl:(c + 1) * sl]
            return carry

        lax.fori_loop(0, (dil * nb) // ATTN_GROUP, group, 0)


def _attention(q, k, v):
    b, _, la, wa = q.shape
    s = la * NRES
    spec = pl.BlockSpec((1, NRES, la, LANES), lambda i, j: (i, 0, 0, j))
    scratch = [pltpu.VMEM((NRES, la, LANES), F32) for _ in range(5)]
    return pl.pallas_call(
        functools.partial(_attn_body, seq=s),
        grid=(b, wa // LANES),
        in_specs=[spec, spec, spec],
        out_specs=pl.BlockSpec((1, s, LANES), lambda i, j: (i, 0, j)),
        out_shape=jax.ShapeDtypeStruct((b, s, wa), F32),
        scratch_shapes=scratch,
        compiler_params=_cparams(2),
        name="dilated_attn",
    )(q, k, v)


def _mlstm_body(q_ref, k_ref, v_ref, g_ref, sq_ref, skn_ref, svn_ref, skt_ref, svt_ref,
                h_ref, c_ref, n_ref, m_ref, so_ref, cst_ref, nst_ref, mst_ref, gs_ref, *, nchunks, nheads):
    _sample_attn_rows(sq_ref, skn_ref, svn_ref, skt_ref, svt_ref, so_ref)
    L = CHUNK
    npairs = nheads // 2
    row = lax.broadcasted_iota(jnp.int32, (L, L), 0)
    lane = lax.broadcasted_iota(jnp.int32, (L, L), 1)
    lane_h = lax.broadcasted_iota(jnp.int32, (nheads, L), 1)
    causal_t = row <= lane
    head0_lane = lane < HEAD_DIM
    head0_row = row < HEAD_DIM
    blockdiag = head0_row == head0_lane
    lane8 = lax.broadcasted_iota(jnp.int32, (8, L), 1)
    row8 = lax.broadcasted_iota(jnp.int32, (8, L), 0)
    pad = jnp.zeros((L - 2 * nheads, L), F32)

    @pl.when(pl.program_id(1) == 0)
    def _():
        cst_ref[...] = jnp.zeros_like(cst_ref)
        nst_ref[...] = jnp.zeros_like(nst_ref)
        mst_ref[...] = jnp.zeros_like(mst_ref)

    def scan_lanes(x, op, fill):
        sh = 1
        while sh < L:
            x = op(x, jnp.where(lane_h >= sh, pltpu.roll(x, sh, 1), fill))
            sh *= 2
        return x

    m_prev = mst_ref[...]
    for c in range(nchunks):
        sl = slice(c * L, (c + 1) * L)
        b_c = scan_lanes(g_ref[0, nheads:2 * nheads, sl], jnp.add, 0.0)
        g_c = g_ref[0, 0:nheads, sl] - b_c
        cm_c = scan_lanes(g_c, jnp.maximum, NEG_INF)
        mm_c = jnp.maximum(cm_c, m_prev)
        mm_last = jnp.maximum(jnp.broadcast_to(cm_c[:, L - 1:L], (nheads, L)), m_prev)
        gs_ref[0, :, sl] = g_c
        gs_ref[1, :, sl] = mm_c
        gs_ref[2, :, sl] = jnp.exp(m_prev - mm_c)
        gs_ref[3, :, sl] = jnp.exp(-(b_c + mm_c))
        gs_ref[4, :, sl] = jnp.exp(g_c - mm_last)
        m_prev = jnp.broadcast_to(b_c[:, L - 1:L], (nheads, L)) + mm_last
    mst_ref[...] = m_prev

    def chunk(c, carry):
        t0 = pl.multiple_of(c * L, L)
        g_all = gs_ref[0, :, pl.ds(t0, L)]
        mm_all = gs_ref[1, :, pl.ds(t0, L)]
        iw_all = gs_ref[2, :, pl.ds(t0, L)]
        en_all = gs_ref[3, :, pl.ds(t0, L)]
        ws_all = gs_ref[4, :, pl.ds(t0, L)]
        cols = jnp.concatenate([g_all, ws_all, pad], axis=0).T
        for p in range(npairs):
            lanes = slice(p * LANES, (p + 1) * LANES)
            h0, h1 = 2 * p, 2 * p + 1
            q = q_ref[0, pl.ds(t0, L), lanes]
            k = k_ref[0, pl.ds(t0, L), lanes]
            v = v_ref[0, pl.ds(t0, L), lanes]
            zero = jnp.zeros_like(q)
            qq = jnp.concatenate([jnp.where(head0_lane, q, zero), jnp.where(head0_lane, zero, q)], axis=0)
            qk = _dot_nt(k, qq)
            cst = cst_ref[p]
            nst = nst_ref[p]
            sc, rs = [], []
            for h, hh in enumerate((h0, h1)):
                d = jnp.where(causal_t, cols[:, hh:hh + 1] - mm_all[hh:hh + 1, :], NEG_INF)
                s_h = qk[:, h * L:(h + 1) * L] * jnp.exp(d)
                sc.append(s_h.astype(BF16))
                rs.append(jnp.sum(s_h, axis=0, keepdims=True))
            ws_l = jnp.where(head0_lane, cols[:, nheads + h0:nheads + h0 + 1], cols[:, nheads + h1:nheads + h1 + 1])
            kws = k.astype(F32) * ws_l
            u = _dot_tn(v, jnp.concatenate(sc + [kws.astype(BF16)], axis=1))
            intra = jnp.where(head0_row, u[:, :L], u[:, L:2 * L])
            inter = _dot_nt(cst.astype(BF16), q)
            qn = _dot_nt(nst.astype(BF16), q)
            iw_l = jnp.where(head0_row, iw_all[h0:h0 + 1, :], iw_all[h1:h1 + 1, :])
            den0 = iw_all[h0:h0 + 1, :] * qn[0:1, :] + rs[0]
            den1 = iw_all[h1:h1 + 1, :] * qn[1:2, :] + rs[1]
            lim = jnp.maximum(jnp.abs(jnp.where(head0_row, den0, den1)),
                              jnp.where(head0_row, en_all[h0:h0 + 1, :], en_all[h1:h1 + 1, :]))
            h_ref[0, pl.ds(t0, L), lanes] = ((iw_l * inter + intra) / lim).T

            cw0 = iw_all[h0:h0 + 1, L - 1:L]
            cw1 = iw_all[h1:h1 + 1, L - 1:L]
            cst_ref[p] = jnp.where(head0_row, cw0, cw1) * cst + jnp.where(blockdiag, u[:, 2 * L:], 0.0)
            ksum = jnp.sum(kws, axis=0, keepdims=True)
            own = (row8 == 0) == (lane8 < HEAD_DIM)
            nst_ref[p] = jnp.where((row8 < 2) & own, jnp.where(row8 == 0, cw0, cw1) * nst + ksum, 0.0)
        return carry

    lax.fori_loop(0, nchunks, chunk, 0, unroll=2)

    @pl.when(pl.program_id(1) == pl.num_programs(1) - 1)
    def _():
        for p in range(npairs):
            for h in range(2):
                sl = slice(h * HEAD_DIM, (h + 1) * HEAD_DIM)
                c_ref[0, 2 * p + h] = cst_ref[p, sl, sl]
                n_ref[0, 2 * p + h:2 * p + h + 1, :] = nst_ref[p, h:h + 1, sl]
        m_ref[0] = mst_ref[...]


def _mlstm(q, k, v, gates, sq, skn, svn, ckt, cvt, layer, *, ts):
    b, s, wb = q.shape
    nheads = wb // HEAD_DIM
    nj = s // ts
    dec_b, nh_a, dh = sq.shape
    per_step = dec_b // (b * nj)
    assert per_step * b * nj == dec_b
    spec = pl.BlockSpec((1, ts, wb), lambda i, j: (i, j, 0))
    srow = pl.BlockSpec((per_step, nh_a, dh), lambda i, j: (i * nj + j, 0, 0))
    cache = pl.BlockSpec((None, per_step, nh_a, dh, ckt.shape[-1]), lambda i, j: (layer, i * nj + j, 0, 0, 0))
    out_shape = (jax.ShapeDtypeStruct((b, s, wb), F32),
                 jax.ShapeDtypeStruct((b, nheads, HEAD_DIM, HEAD_DIM), F32),
                 jax.ShapeDtypeStruct((b, nheads, HEAD_DIM), F32),
                 jax.ShapeDtypeStruct((b, nheads, LANES), F32),
                 jax.ShapeDtypeStruct((dec_b, nh_a, dh), F32))
    out_specs = (spec,
                 pl.BlockSpec((1, nheads, HEAD_DIM, HEAD_DIM), lambda i, j: (i, 0, 0, 0)),
                 pl.BlockSpec((1, nheads, HEAD_DIM), lambda i, j: (i, 0, 0)),
                 pl.BlockSpec((1, nheads, LANES), lambda i, j: (i, 0, 0)),
                 srow)
    return pl.pallas_call(
        functools.partial(_mlstm_body, nchunks=ts // CHUNK, nheads=nheads),
        grid=(b, nj),
        in_specs=[spec, spec, spec, pl.BlockSpec((1, 2 * nheads, ts), lambda i, j: (i, 0, j)),
                  srow, srow, srow, cache, cache],
        out_specs=out_specs,
        out_shape=out_shape,
        scratch_shapes=[pltpu.VMEM((nheads // 2, LANES, LANES), F32), pltpu.VMEM((nheads // 2, 8, LANES), F32),
                        pltpu.VMEM((nheads, LANES), F32), pltpu.VMEM((5, nheads, ts), F32)],
        compiler_params=_cparams(2),
        name="mlstm",
    )(q, k, v, gates, sq, skn, svn, ckt, cvt)


def _post_body(*refs, prompt, seq_tiles, ck):
    (x_ref, oa_ref, hb_ref, og_ref, p_ref, ga_ref, gb_ref, wo_ref, g_ref, wup_ref, cw_ref, cb_ref, wdn_ref,
     gp_ref, wpg_ref, wpp_ref) = refs[:16]
    if prompt:
        o_ref, conv_ref, y_ref, carry_ref = refs[16:]
    else:
        p0_ref, p1_ref, o_ref, conv_ref, y_ref = refs[16:]
    tm = x_ref.shape[0]
    dff = wdn_ref.shape[0]
    wa = oa_ref.shape[1]
    ya = _rms(oa_ref[...], ga_ref[...]).astype(BF16)
    yb = (og_ref[...] * _rms(hb_ref[...], gb_ref[...])).astype(BF16)
    x = x_ref[...] + _dot(ya, wo_ref[0:wa, :]) + _dot(yb, wo_ref[wa:, :])
    h = _rms(x, g_ref[...]).astype(BF16)
    if prompt:
        @pl.when(pl.program_id(0) % seq_tiles == 0)
        def _():
            carry_ref[...] = jnp.zeros_like(carry_ref)
        row = lax.broadcasted_iota(jnp.int32, (tm, ck), 0)
    for c in range(dff // ck):
        cs = slice(c * ck, (c + 1) * ck)
        gate = _dot(h, wup_ref[:, c * ck:(c + 1) * ck])
        val = _dot(h, wup_ref[:, dff + c * ck:dff + (c + 1) * ck])
        if prompt:
            b0 = carry_ref[0:1, cs]
            b1 = carry_ref[1:2, cs]
            g1 = jnp.where(row == 0, b1, pltpu.roll(gate, 1, 0))
            g2 = jnp.where(row == 0, b0, jnp.where(row == 1, b1, pltpu.roll(gate, 2, 0)))
            carry_ref[:, cs] = gate[tm - 2:tm, :]
        else:
            g2 = p0_ref[:, cs]
            g1 = p1_ref[:, cs]
            conv_ref[:, cs] = gate
        gc = cb_ref[:, cs] + (cw_ref[0:1, cs] * g2 + cw_ref[1:2, cs] * g1 + cw_ref[2:3, cs] * gate)
        y_ref[:, cs] = (_gelu_tanh(gc) * val).astype(BF16)
    if prompt:
        conv_ref[0] = carry_ref[...]
    x = x + _dot(y_ref[...], wdn_ref[...])
    gate = jax.nn.sigmoid(_dot(_rms(x, gp_ref[...]).astype(BF16), wpg_ref[...]))
    o_ref[...] = x + gate * _dot(p_ref[...].astype(BF16), wpp_ref[...])


def _post(x, oa, hb, og, p, layer, params, prev=None, *, tm, seq_tiles=1, batch=1):
    t, d = x.shape
    dff = params[7].shape[1]
    prompt = prev is None
    tok = lambda w_: pl.BlockSpec((tm, w_), lambda i: (i, 0))
    in_specs = [tok(d), tok(oa.shape[1]), tok(hb.shape[1]), tok(og.shape[1]),
                pl.BlockSpec((None, tm, p.shape[2]), lambda i: (layer, i, 0))]
    in_specs += [_layer_spec(a, layer) for a in params]
    args = [x, oa, hb, og, p, *params]
    scratch = [pltpu.VMEM((tm, dff), BF16)]
    if prompt:
        conv_shape = jax.ShapeDtypeStruct((batch, 2, dff), F32)
        conv_spec = pl.BlockSpec((1, 2, dff), lambda i: (i // seq_tiles, 0, 0))
        scratch.append(pltpu.VMEM((2, dff), F32))
    else:
        in_specs += [tok(dff), tok(dff)]
        args += list(prev)
        conv_shape = jax.ShapeDtypeStruct((t, dff), F32)
        conv_spec = tok(dff)
    return pl.pallas_call(
        functools.partial(_post_body, prompt=prompt, seq_tiles=seq_tiles, ck=FFN_CHUNK),
        grid=(t // tm,),
        in_specs=in_specs,
        out_specs=(tok(d), conv_spec),
        out_shape=(jax.ShapeDtypeStruct((t, d), F32), conv_shape),
        scratch_shapes=scratch,
        compiler_params=_cparams(1),
        name="post_mixer",
    )(*args)


def _sample_attn_rows(q_ref, kn_ref, vn_ref, kt_ref, vt_ref, o_ref):
    nrows, nh, _, wbuf = kt_ref.shape
    r = lax.broadcasted_iota(jnp.int32, (HEAD_DIM, HEAD_DIM), 0)
    c = lax.broadcasted_iota(jnp.int32, (HEAD_DIM, HEAD_DIM), 1)
    diag = r == c
    dist = wbuf - lax.broadcasted_iota(jnp.int32, (1, wbuf), 1)
    for b in range(nrows):
        q = q_ref[b]
        kn = kn_ref[b]
        vn = vn_ref[b]
        s_new = jnp.sum(q * kn, axis=-1, keepdims=True)
        rows = []
        for h in range(nh):
            q_col = jnp.sum(jnp.where(diag, q[h:h + 1, :], 0.0), axis=1, keepdims=True)
            rows.append(jnp.sum(kt_ref[b, h] * q_col, axis=0, keepdims=True))
        s = jnp.concatenate(rows, axis=0)
        stats = []
        for dil in DILATIONS:
            mask = ((dist & (dil - 1)) == 0) & (dist <= BAND * dil)
            sg = jnp.where(mask, s, NEG_INF)
            m = jnp.maximum(jnp.max(sg, axis=1, keepdims=True), s_new)
            e = jnp.exp(sg - m)
            e_new = jnp.exp(s_new - m)
            stats.append((m, e, e_new, jnp.sum(e, axis=1, keepdims=True) + e_new))
        mm = jnp.maximum(jnp.maximum(stats[0][0], stats[1][0]), stats[2][0])
        cs = [jnp.exp(m - mm) for m, _, _, _ in stats]
        wgt = cs[0] * stats[0][1] + cs[1] * stats[1][1] + cs[2] * stats[2][1]
        w_new = cs[0] * stats[0][2] + cs[1] * stats[1][2] + cs[2] * stats[2][2]
        den = cs[0] * stats[0][3] + cs[1] * stats[1][3] + cs[2] * stats[2][3]
        outs = []
        for h in range(nh):
            o_col = jnp.sum(vt_ref[b, h] * wgt[h:h + 1, :], axis=1, keepdims=True)
            outs.append(jnp.sum(jnp.where(diag, o_col, 0.0), axis=0, keepdims=True))
        o_ref[b] = (jnp.concatenate(outs, axis=0) + w_new * vn) / den


def _smlstm_body(q_ref, k_ref, v_ref, ig_ref, lf_ref, c_ref, n_ref, m_ref, *refs):
    h_ref, co_ref, no_ref, mo_ref = refs[-4:]
    r = lax.broadcasted_iota(jnp.int32, (HEAD_DIM, HEAD_DIM), 0)
    l = lax.broadcasted_iota(jnp.int32, (HEAD_DIM, HEAD_DIM), 1)
    nh = q_ref.shape[1]
    for b in range(q_ref.shape[0]):
        q, k, v = q_ref[b], k_ref[b], v_ref[b]
        ig, lf, m = ig_ref[b], lf_ref[b], m_ref[b]
        n = n_ref[b]
        inter = lf + m
        m_t = jnp.maximum(inter, ig)
        dw = jnp.exp(ig - m_t)
        iw = jnp.exp(inter - m_t)
        sc = jnp.sum(q * k, axis=-1, keepdims=True) * dw
        cq = []
        for h in range(nh):
            c_h = c_ref[b, h]
            q_h = jnp.broadcast_to(q[h:h + 1, :], (8, HEAD_DIM)).astype(BF16)
            cq.append(_dot_nt(q_h, c_h.astype(BF16))[0:1, :])
            v_col = jnp.sum(jnp.where(r == l, v[h:h + 1, :], 0.0), axis=1, keepdims=True)
            co_ref[b, h] = iw[h:h + 1, :] * c_h + (dw[h:h + 1, :] * v_col) * k[h:h + 1, :]
        num = iw * jnp.concatenate(cq, axis=0) + sc * v
        den = iw * jnp.sum(n * q, axis=-1, keepdims=True) + sc
        h_ref[b] = num / jnp.maximum(jnp.abs(den), jnp.exp(-m_t))
        no_ref[b] = iw * n + dw * k
        mo_ref[b] = m_t


def _sample_mlstm(q, k, v, ig, lf, c, n, m, layer, c_acc, *, rows):
    b, nh, dh = q.shape
    depth = c.shape[0]
    row = pl.BlockSpec((rows, nh, dh), lambda i: (i, 0, 0))
    col = pl.BlockSpec((rows, nh, 1), lambda i: (i, 0, 0))
    lrow = pl.BlockSpec((None, rows, nh, dh), lambda i: (layer, i, 0, 0))
    lcol = pl.BlockSpec((None, rows, nh, 1), lambda i: (layer, i, 0, 0))
    lcst = pl.BlockSpec((None, rows, nh, dh, dh), lambda i: (layer, i, 0, 0, 0))
    args = [q, k, v, ig, lf, c, n, m]
    in_specs = [row, row, row, col, col, lcst, lrow, lcol]
    kwargs = {}
    if c_acc is not None:
        kwargs["input_output_aliases"] = {len(args): 1}
        args.append(c_acc)
        in_specs.append(pl.BlockSpec(memory_space=pl.ANY))
    return pl.pallas_call(
        _smlstm_body,
        grid=(b // rows,),
        in_specs=in_specs,
        out_specs=(row, lcst, row, col),
        out_shape=(jax.ShapeDtypeStruct((b, nh, dh), F32), jax.ShapeDtypeStruct((depth, b, nh, dh, dh), F32),
                   jax.ShapeDtypeStruct((b, nh, dh), F32), jax.ShapeDtypeStruct((b, nh, 1), F32)),
        compiler_params=_cparams(1),
        name="sample_mlstm",
        **kwargs,
    )(*args)


def _rope_tables(pos):
    half = ROT_DIM // 2
    inv = ROPE_THETA ** (-jnp.arange(half, dtype=F32) / half)
    ang = pos.astype(F32)[:, None] * inv[None, :]
    cos, sin = jnp.cos(ang), jnp.sin(ang)
    n = pos.shape[0]
    rest = HEAD_DIM - ROT_DIM
    c = jnp.concatenate([cos, cos, jnp.ones((n, rest), F32)], axis=1)
    sa = jnp.concatenate([-sin, jnp.zeros((n, half + rest), F32)], axis=1)
    sb = jnp.concatenate([jnp.zeros((n, half), F32), sin, jnp.zeros((n, rest), F32)], axis=1)
    rep = LANES // HEAD_DIM
    return tuple(jnp.concatenate([t] * rep, axis=1) for t in (c, sa, sb))


def _rope_angles_t(pos):
    half = ROT_DIM // 2
    inv = ROPE_THETA ** (-jnp.arange(half, dtype=F32) / half)
    ang = pos.astype(F32)[:, None] * inv[None, :]
    return jnp.cos(ang).T, jnp.sin(ang).T


def kernel(x_prompt, x_sample, cache_win_k, cache_win_v, state_C, state_n, state_m, state_conv, p_prompt, p_sample,
           norm_mix, w_in, q_norm, k_norm, b_gates, out_norm_a, out_norm_b, w_out, norm_ffn, w_up, conv_w, conv_b,
           w_down, norm_ple, w_ple_gate, w_ple_proj):
    batch, seq, d = x_prompt.shape
    dec_batch, dec_seq, _ = x_sample.shape
    depth = w_in.shape[0]
    nh_a = cache_win_k.shape[3]
    nh_b = state_C.shape[2]
    wa, wb = nh_a * HEAD_DIM, nh_b * HEAD_DIM
    wbuf = cache_win_k.shape[2]
    win_max = BAND * DILATIONS[-1]
    keep = min(win_max, seq)
    assert dec_seq == 1 and wbuf == win_max and seq % win_max == 0
    assert w_in.shape[2] == 3 * wa + 4 * wb + 2 * nh_b and wa % LANES == 0 and wb % LANES == 0

    tm = min(TOKEN_TILE, seq)
    seq_tiles = seq // tm
    tp, ts = batch * seq, dec_batch * dec_seq

    cut = 3 * wa + 4 * wb
    w_in_t = jnp.transpose(w_in, (0, 2, 1))
    w_main = w_in_t[:, :cut, :].astype(BF16)
    w_gate = w_in_t[:, cut + 2 * nh_b - LANES:, :].astype(BF16)
    bg = jnp.concatenate([jnp.zeros((depth, GATE_LANE0), F32), b_gates], axis=1)[:, None, :]
    qn = jnp.tile(q_norm, (1, nh_a))[:, None, :]
    kn = jnp.tile(k_norm, (1, nh_a))[:, None, :]
    rows = lambda a: a[:, None, :]
    cols = lambda a: jnp.broadcast_to(a[:, 0, :, None], (depth, wa, LANES))
    inproj_params_s = (rows(norm_mix), w_main, w_gate, qn, kn, bg)
    inproj_params_p = (rows(norm_mix), w_main, w_gate, cols(qn), cols(kn), bg)
    post_params = (rows(out_norm_a), rows(out_norm_b), w_out.astype(BF16), rows(norm_ffn), w_up.astype(BF16),
                   conv_w, rows(conv_b), w_down.astype(BF16), rows(norm_ple), w_ple_gate.astype(BF16),
                   w_ple_proj.astype(BF16))
    pp = p_prompt.reshape(depth, tp, -1)
    ps = p_sample.reshape(depth, ts, -1)
    head_of_lane = jnp.arange(wa) // HEAD_DIM
    ind = (head_of_lane[:, None] == jnp.arange(LANES)[None, :]).astype(BF16)
    indt = ind.T
    tab_p = _rope_angles_t(jnp.arange(seq, dtype=jnp.int32))
    tab_s = tuple(jnp.broadcast_to(t, (ts, LANES))
                  for t in _rope_tables(PAST_LEN + jnp.arange(dec_seq, dtype=jnp.int32)))

    ckt = jnp.transpose(cache_win_k, (0, 1, 3, 4, 2))
    cvt = jnp.transpose(cache_win_v, (0, 1, 3, 4, 2))

    xp = x_prompt.reshape(tp, d)
    xs = x_sample.reshape(ts, d)
    outs = [[] for _ in range(12)]
    wk_acc = wv_acc = cs_acc = None
    state_m4 = state_m[..., None]
    for i in range(depth):
        q_rm, wk_acc, wv_acc, qbp, kbp, vbp, ogp, gt, k_rm, v_rm = _inproj(
            xp, i, *inproj_params_p, tab_p, (),
            tm=tm, seq_tiles=seq_tiles, batch=batch, mdtype=BF16, transpose_gates=True,
            window=(keep, depth, wk_acc, wv_acc))
        qa, ka, va, qb, kb, vb, og, ga = _inproj(
            xs, i, *inproj_params_s, tab_s, (ind, indt),
            tm=ts, seq_tiles=1, batch=dec_batch, mdtype=F32, transpose_gates=False)
        r3 = lambda a: a.reshape(batch, seq, a.shape[-1])
        hd = lambda a: a.reshape(dec_batch, -1, HEAD_DIM)
        oap = _attention(q_rm, k_rm, v_rm)
        hbp, c_new, n_new, m_new, oa = _mlstm(r3(qbp), r3(kbp), r3(vbp), gt, hd(qa), hd(ka), hd(va), ckt, cvt, i,
                                              ts=min(seq, MLSTM_TILE))
        xp, conv_new = _post(xp, oap.reshape(tp, wa), hbp.reshape(tp, wb), ogp, pp, i, post_params,
                             tm=tm, seq_tiles=seq_tiles, batch=batch)
        outs[2].append(c_new)
        outs[3].append(n_new)
        outs[4].append(m_new[..., 0])
        outs[5].append(conv_new)
        ig = ga[:, GATE_LANE0:GATE_LANE0 + nh_b, None]
        lf = ga[:, GATE_LANE0 + nh_b:, None]
        hb, cs_acc, n_new, m_new = _sample_mlstm(hd(qb), hd(kb), hd(vb), ig, lf, state_C, state_n, state_m4, i,
                                                 cs_acc, rows=SAMPLE_ROWS)
        xs, gate_new = _post(xs, oa.reshape(ts, wa), hb.reshape(ts, wb), og, ps, i, post_params,
                             prev=(state_conv[i][:, 0], state_conv[i][:, 1]), tm=ts)
        outs[6].append(ka.reshape(dec_batch, dec_seq, nh_a, HEAD_DIM))
        outs[7].append(va.reshape(dec_batch, dec_seq, nh_a, HEAD_DIM))
        outs[9].append(n_new)
        outs[10].append(m_new[..., 0])
        outs[11].append(jnp.stack([state_conv[i][:, 1], gate_new], axis=1))
    def window_out(acc):
        return jnp.transpose(acc.reshape(depth, batch, nh_a, HEAD_DIM, keep), (0, 1, 4, 2, 3))

    st = jnp.stack
    return ((xp.reshape(batch, seq, d), xs.reshape(dec_batch, dec_seq, d), window_out(wk_acc), window_out(wv_acc))
            + tuple(st(o) for o in outs[2:8]) + (cs_acc,) + tuple(st(o) for o in outs[9:]))
```

```python
import functools

import jax
import jax.numpy as jnp
from jax import lax
from jax.experimental import pallas as pl
from jax.experimental.pallas import tpu as pltpu

F32 = jnp.float32
BF16 = jnp.bfloat16

HEAD_DIM = 64
ROT_DIM = HEAD_DIM // 4
ROPE_THETA = 500000.0
RMS_EPS = 1e-6
NEG_INF = -1e30
PAST_LEN = 8192
DILATIONS = (1, 4, 16)
NRES = 4
BAND = 128
ATTN_GROUP = 16
CHUNK = 128
FFN_CHUNK = 256
TOKEN_TILE = 512
MLSTM_TILE = 1024
SAMPLE_ROWS = 8
LANES = 128
GATE_LANE0 = LANES - 16
VMEM_LIMIT = 56 * 1024 * 1024


def _cparams(n_axes):
    return pltpu.CompilerParams(dimension_semantics=("arbitrary",) * n_axes,
                                vmem_limit_bytes=VMEM_LIMIT)


def _const_spec(shape):
    return pl.BlockSpec(shape, lambda *_: (0,) * len(shape))


def _layer_spec(stack, layer):
    nd = stack.ndim
    return pl.BlockSpec((None,) + stack.shape[1:], lambda *_: (layer,) + (0,) * (nd - 1),
                        pipeline_mode=pl.Buffered(1))


def _rms(x, g):
    return (x * lax.rsqrt(jnp.mean(x * x, axis=-1, keepdims=True) + RMS_EPS)) * g


def _dot(a, b):
    return jnp.dot(a, b, preferred_element_type=F32)


def _dot_nt(a, b):
    return lax.dot_general(a, b, (((1,), (1,)), ((), ())), preferred_element_type=F32)


def _dot_tn(a, b):
    return lax.dot_general(a, b, (((0,), (0,)), ((), ())), preferred_element_type=F32)


def _split_dot(a, b):
    hi = a.astype(BF16)
    lo = (a - hi.astype(F32)).astype(BF16)
    return _dot(hi, b) + _dot(lo, b)


def _log_sigmoid(x):
    return jnp.minimum(x, 0.0) - jnp.log1p(jnp.exp(-jnp.abs(x)))


def _gelu_tanh(x):
    c = 0.7978845608028654
    return x * (0.5 * (1.0 + jnp.tanh(c * (x + 0.044715 * (x * x * x)))))


def _inproj_body(x_ref, g_ref, w_ref, wg_ref, qn_ref, kn_ref, bg_ref, *refs, wa, transpose_gates, n_acc=0):
    h = _rms(x_ref[...], g_ref[...]).astype(BF16)
    proj = lambda lo, hi: _dot_nt(h, w_ref[lo:hi, :])
    reps = wa // LANES
    half = ROT_DIM // 2
    if transpose_gates:
        cos_ref, sin_ref = refs[:2]
        qa_ref, ka_ref, va_ref, qb_ref, kb_ref, vb_ref, og_ref, gt_ref, *rest = refs[2 + n_acc:]
        tiles = x_ref.shape[0] // LANES

        def head_norm_rope(z, gain_ref):
            zt = z.T
            cos, sin = cos_ref[...], sin_ref[...]
            heads = []
            for hh in range(wa // HEAD_DIM):
                x = zt[hh * HEAD_DIM:(hh + 1) * HEAD_DIM, :]
                gain = jnp.concatenate([gain_ref[hh * HEAD_DIM:(hh + 1) * HEAD_DIM, :]] * tiles, axis=1)
                xn = (x * lax.rsqrt(jnp.mean(x * x, axis=0, keepdims=True) + RMS_EPS)) * gain
                x1, x2 = xn[0:half], xn[half:2 * half]
                heads.append(jnp.concatenate([x1 * cos - x2 * sin, x2 * cos + x1 * sin, xn[2 * half:]], axis=0))
            zt = jnp.concatenate(heads, axis=0)
            return zt, zt.T
    else:
        cos_ref, sa_ref, sb_ref, ind_ref, indt_ref = refs[:5]
        qa_ref, ka_ref, va_ref, qb_ref, kb_ref, vb_ref, og_ref, gt_ref, *rest = refs[5:]
        cos = jnp.concatenate([cos_ref[...]] * reps, axis=1)
        sa = jnp.concatenate([sa_ref[...]] * reps, axis=1)
        sb = jnp.concatenate([sb_ref[...]] * reps, axis=1)
        ind = ind_ref[...]
        indt = indt_ref[...]

        def head_norm_rope(z, gain_ref):
            ss = _split_dot(z * z, ind)
            inv = lax.rsqrt(ss * (1.0 / HEAD_DIM) + RMS_EPS)
            zn = (z * _split_dot(inv, indt)) * gain_ref[...]
            return None, zn * cos + pltpu.roll(zn, wa - half, 1) * sa + pltpu.roll(zn, half, 1) * sb

    qa = head_norm_rope(proj(0, wa), qn_ref)[1] * (HEAD_DIM ** -0.5)
    kt, ka = head_norm_rope(proj(wa, 2 * wa), kn_ref)
    va = proj(2 * wa, 3 * wa)
    if transpose_gates:
        ka_ref[...] = kt
        va_ref[...] = va.T
        krm_ref, vrm_ref, zs_ref = rest
        rows = zs_ref.shape[1] // NRES
        for val, dst in ((qa, qa_ref), (ka, krm_ref), (va, vrm_ref)):
            for c in range(reps):
                zs_ref[c] = val[:, c * LANES:(c + 1) * LANES]
            for r in range(NRES):
                for c in range(reps):
                    dst[0, r, :, c * LANES:(c + 1) * LANES] = zs_ref[c, pl.ds(r, rows, stride=NRES), :]
    else:
        qa_ref[...] = qa
        ka_ref[...] = ka
        va_ref[...] = va
    wb = (w_ref.shape[0] - 3 * wa) // 4
    o = 3 * wa
    qb_ref[...] = proj(o, o + wb).astype(qb_ref.dtype)
    kb_ref[...] = (proj(o + wb, o + 2 * wb) * (HEAD_DIM ** -0.5)).astype(kb_ref.dtype)
    vb_ref[...] = proj(o + 2 * wb, o + 3 * wb).astype(vb_ref.dtype)
    og_ref[...] = jax.nn.sigmoid(proj(o + 3 * wb, o + 4 * wb))

    v = _dot_nt(h, wg_ref[...]) + bg_ref[...]
    lane = lax.broadcasted_iota(jnp.int32, v.shape, 1)
    ga = jnp.where(lane >= GATE_LANE0 + 8, _log_sigmoid(v), v)
    if transpose_gates:
        gt_ref[0] = ga.T[GATE_LANE0:, :]
    else:
        gt_ref[...] = ga


def _inproj(x, layer, gain, w_main, w_gate, qn, kn, bg, tables, consts, *, tm, seq_tiles, batch, mdtype,
            transpose_gates, window=None):
    t, d = x.shape
    wa = qn.shape[1] if transpose_gates else qn.shape[2]
    wb = (w_main.shape[1] - 3 * wa) // 4
    nt = t // tm
    tok = lambda w: pl.BlockSpec((tm, w), lambda i: (i, 0))
    sds = jax.ShapeDtypeStruct
    args = [x, gain, w_main, w_gate, qn, kn, bg, *tables, *consts]
    in_specs = [tok(d)] + [_layer_spec(a, layer) for a in (gain, w_main, w_gate, qn, kn, bg)]
    tail_shape = [sds((t, wb), mdtype), sds((t, wb), mdtype), sds((t, wb), mdtype), sds((t, wb), F32)]
    tail_specs = [tok(wb), tok(wb), tok(wb), tok(wb)]
    kwargs = {}
    if transpose_gates:
        seq = seq_tiles * tm
        keep, depth, k_acc, v_acc = window
        first = (seq - keep) // tm
        in_specs += [pl.BlockSpec((tables[0].shape[0], tm), lambda i: (0, i % seq_tiles))] * len(tables)
        rm_shape = sds((batch, NRES, seq // NRES, wa), F32)
        rm_spec = pl.BlockSpec((1, NRES, tm // NRES, wa), lambda i: (i // seq_tiles, 0, i % seq_tiles, 0))
        win_shape = sds((depth, batch, wa, keep), F32)
        win_spec = pl.BlockSpec((None, None, wa, tm),
                                lambda i: (layer, i // seq_tiles, 0, jnp.maximum(i % seq_tiles - first, 0)))
        gt_shape = sds((batch, 16, seq), F32)
        gt_spec = pl.BlockSpec((1, 16, tm), lambda i: (i // seq_tiles, 0, i % seq_tiles))
        out_shape = [rm_shape, win_shape, win_shape] + tail_shape + [gt_shape, rm_shape, rm_shape]
        out_specs = [rm_spec, win_spec, win_spec] + tail_specs + [gt_spec, rm_spec, rm_spec]
        scratch = [pltpu.VMEM((wa // LANES, tm, LANES), F32)]
        kwargs["input_output_aliases"] = {len(args): 1, len(args) + 1: 2}
        args += [k_acc, v_acc]
        in_specs += [pl.BlockSpec(memory_space=pl.ANY)] * 2
        body = functools.partial(_inproj_body, wa=wa, transpose_gates=True, n_acc=2)
    else:
        in_specs += [pl.BlockSpec((tm, LANES), lambda i: (i % seq_tiles, 0))] * len(tables)
        in_specs += [_const_spec(c.shape) for c in consts]
        out_shape = [sds((t, wa), F32)] * 3 + tail_shape + [sds((t, LANES), F32)]
        out_specs = [tok(wa)] * 3 + tail_specs + [tok(LANES)]
        scratch = []
        body = functools.partial(_inproj_body, wa=wa, transpose_gates=False)
    return pl.pallas_call(
        body,
        grid=(nt,),
        in_specs=in_specs,
        out_specs=tuple(out_specs),
        out_shape=tuple(out_shape),
        scratch_shapes=scratch,
        compiler_params=_cparams(1),
        name="inproj",
        **kwargs,
    )(*args)


def _attn_body(q_ref, k_ref, v_ref, o_ref, acc_ref, m0_ref, m1_ref, l0_ref, l1_ref, *, seq):
    head0 = lax.broadcasted_iota(jnp.int32, (BAND, LANES), 1) < HEAD_DIM
    qrow = lax.broadcasted_iota(jnp.int32, (2 * BAND, 2 * BAND), 0) % BAND
    kcol = lax.broadcasted_iota(jnp.int32, (2 * BAND, 2 * BAND), 1)
    in_cur = kcol >= BAND
    qrow1 = lax.broadcasted_iota(jnp.int32, (2 * BAND, BAND), 0) % BAND
    kcol1 = lax.broadcasted_iota(jnp.int32, (2 * BAND, BAND), 1)

    for pi, dil in enumerate(DILATIONS):
        nb = seq // (dil * BAND)
        first, last = pi == 0, pi == len(DILATIONS) - 1
        grp = min(ATTN_GROUP, nb)
        nres = ATTN_GROUP // grp
        ngrp = nb // grp
        u = max(NRES // dil, 1)
        sl = BAND // u
        stride = max(dil // NRES, 1)
        assert not last or u == 1
        pos = lambda i, u=u, sl=sl: u * (i % sl) + i // sl
        dist = (BAND + pos(qrow)) - (BAND * (kcol // BAND) + pos(kcol % BAND))
        band = (dist >= 0) & (dist <= BAND)
        tri = pos(kcol1) <= pos(qrow1)

        def group(idx, carry, dil=dil, first=first, last=last, grp=grp, nres=nres, ngrp=ngrp, u=u, sl=sl,
                  stride=stride, band=band, tri=tri):
            rr, gi = (idx // ngrp, idx % ngrp) if ngrp > 1 else (idx, 0)

            def rows_of(r_d, blk, n):
                if dil <= NRES:
                    a = blk * sl
                    a = a if isinstance(a, int) else pl.multiple_of(a, 8)
                    return [(dil * c + r_d, pl.ds(a, n * sl)) for c in range(u)]
                return [(r_d % NRES, pl.ds(stride * BAND * blk + r_d // NRES, n * BAND, stride=stride))]

            def slabs(ref, r_d, blk, n):
                parts = [ref[i0, rows, :] for i0, rows in rows_of(r_d, blk, n)]
                return [jnp.concatenate([p[j * sl:(j + 1) * sl] for p in parts], axis=0) for j in range(n)]

            tasks = []
            for ri in range(nres):
                r_d = rr * nres + ri
                a0 = gi * grp
                kb = [x.astype(BF16) for x in slabs(k_ref.at[0], r_d, a0, grp)]
                vb = [x.astype(BF16) for x in slabs(v_ref.at[0], r_d, a0, grp)]
                if ngrp > 1:
                    ap = jnp.maximum(a0 - 1, 0)
                    kb.insert(0, slabs(k_ref.at[0], r_d, ap, 1)[0].astype(BF16))
                    vb.insert(0, slabs(v_ref.at[0], r_d, ap, 1)[0].astype(BF16))
                qs = slabs(q_ref.at[0], r_d, a0, grp)
                if not first:
                    olds = [slabs(ref, r_d, a0, grp) for ref in (m0_ref, m1_ref, l0_ref, l1_ref, acc_ref)]
                for j in range(grp):
                    jj = j + (1 if ngrp > 1 else 0)
                    old = None
                    if not first:
                        old = (jnp.concatenate([olds[0][j], olds[1][j]], axis=0),
                               jnp.concatenate([olds[2][j], olds[3][j]], axis=0), olds[4][j])
                    if jj == 0:
                        kk, vv, mask = kb[0], vb[0], tri
                    else:
                        kk = jnp.concatenate([kb[jj - 1], kb[jj]], axis=0)
                        vv = jnp.concatenate([vb[jj - 1], vb[jj]], axis=0)
                        mask = band & (in_cur | (gi > 0)) if (j == 0 and ngrp > 1) else band
                    tasks.append(((r_d, a0 + j), qs[j], kk, vv, mask, old))
            results = []
            for rows_j, q, kk, vv, mask, old in tasks:
                qq = jnp.concatenate([jnp.where(head0, q, 0.0), jnp.where(head0, 0.0, q)], axis=0).astype(BF16)
                s = jnp.where(mask, _dot_nt(qq, kk), NEG_INF)
                rm = jnp.max(s, axis=1, keepdims=True)
                if first:
                    mn = jnp.broadcast_to(rm, (2 * BAND, LANES))
                else:
                    mn = jnp.maximum(old[0], rm)
                p = jnp.exp((s - jnp.concatenate([mn] * (s.shape[1] // LANES), axis=1)).astype(BF16))
                pv = _dot(p, jnp.concatenate([vv, jnp.ones_like(vv)], axis=1))
                ls = pv[:, LANES:]
                acc = jnp.where(head0, pv[:BAND, :LANES], pv[BAND:, :LANES])
                if first:
                    ln = ls
                else:
                    a = jnp.exp(old[0] - mn)
                    ln = a * old[1] + ls
                    acc = jnp.where(head0, a[:BAND], a[BAND:]) * old[2] + acc
                results.append((rows_j, mn, ln, acc))
            for (r_d, blk), mn, ln, acc in results:
                if last:
                    o_ref[0, pl.ds(dil * BAND * blk + r_d, BAND, stride=dil), :] = (
                        acc / jnp.where(head0, ln[:BAND], ln[BAND:]))
                else:
                    for ref, val in ((m0_ref, mn[:BAND]), (m1_ref, mn[BAND:]), (l0_ref, ln[:BAND]),
                                     (l1_ref, ln[BAND:]), (acc_ref, acc)):
                        for c, (i0, rows) in enumerate(rows_of(r_d, blk, 1)):
                            ref[i0, rows, :] = val[c * sl:(c + 1) * sl]
            return carry

        lax.fori_loop(0, (dil * nb) // ATTN_GROUP, group, 0)


def _attention(q, k, v):
    b, _, la, wa = q.shape
    s = la * NRES
    spec = pl.BlockSpec((1, NRES, la, LANES), lambda i, j: (i, 0, 0, j))
    scratch = [pltpu.VMEM((NRES, la, LANES), F32) for _ in range(5)]
    return pl.pallas_call(
        functools.partial(_attn_body, seq=s),
        grid=(b, wa // LANES),
        in_specs=[spec, spec, spec],
        out_specs=pl.BlockSpec((1, s, LANES), lambda i, j: (i, 0, j)),
        out_shape=jax.ShapeDtypeStruct((b, s, wa), F32),
        scratch_shapes=scratch,
        compiler_params=_cparams(2),
        name="dilated_attn",
    )(q, k, v)


def _mlstm_body(q_ref, k_ref, v_ref, g_ref, sq_ref, skn_ref, svn_ref, skt_ref, svt_ref,
                h_ref, c_ref, n_ref, m_ref, so_ref, cst_ref, nst_ref, mst_ref, gs_ref, *, nchunks, nheads):
    _sample_attn_rows(sq_ref, skn_ref, svn_ref, skt_ref, svt_ref, so_ref)
    L = CHUNK
    npairs = nheads // 2
    row = lax.broadcasted_iota(jnp.int32, (L, L), 0)
    lane = lax.broadcasted_iota(jnp.int32, (L, L), 1)
    lane_h = lax.broadcasted_iota(jnp.int32, (nheads, L), 1)
    causal_t = row <= lane
    head0_lane = lane < HEAD_DIM
    head0_row = row < HEAD_DIM
    blockdiag = head0_row == head0_lane
    lane8 = lax.broadcasted_iota(jnp.int32, (8, L), 1)
    row8 = lax.broadcasted_iota(jnp.int32, (8, L), 0)
    pad = jnp.zeros((L - 2 * nheads, L), F32)

    @pl.when(pl.program_id(1) == 0)
    def _():
        cst_ref[...] = jnp.zeros_like(cst_ref)
        nst_ref[...] = jnp.zeros_like(nst_ref)
        mst_ref[...] = jnp.zeros_like(mst_ref)

    def scan_lanes(x, op, fill):
        sh = 1
        while sh < L:
            x = op(x, jnp.where(lane_h >= sh, pltpu.roll(x, sh, 1), fill))
            sh *= 2
        return x

    m_prev = mst_ref[...]
    for c in range(nchunks):
        sl = slice(c * L, (c + 1) * L)
        b_c = scan_lanes(g_ref[0, nheads:2 * nheads, sl], jnp.add, 0.0)
        g_c = g_ref[0, 0:nheads, sl] - b_c
        cm_c = scan_lanes(g_c, jnp.maximum, NEG_INF)
        mm_c = jnp.maximum(cm_c, m_prev)
        mm_last = jnp.maximum(jnp.broadcast_to(cm_c[:, L - 1:L], (nheads, L)), m_prev)
        gs_ref[0, :, sl] = g_c
        gs_ref[1, :, sl] = mm_c
        gs_ref[2, :, sl] = jnp.exp(m_prev - mm_c)
        gs_ref[3, :, sl] = jnp.exp(-(b_c + mm_c))
        gs_ref[4, :, sl] = jnp.exp(g_c - mm_last)
        m_prev = jnp.broadcast_to(b_c[:, L - 1:L], (nheads, L)) + mm_last
    mst_ref[...] = m_prev

    def chunk(c, carry):
        t0 = pl.multiple_of(c * L, L)
        g_all = gs_ref[0, :, pl.ds(t0, L)]
        mm_all = gs_ref[1, :, pl.ds(t0, L)]
        iw_all = gs_ref[2, :, pl.ds(t0, L)]
        en_all = gs_ref[3, :, pl.ds(t0, L)]
        ws_all = gs_ref[4, :, pl.ds(t0, L)]
        cols = jnp.concatenate([g_all, ws_all, pad], axis=0).T
        for p in range(npairs):
            lanes = slice(p * LANES, (p + 1) * LANES)
            h0, h1 = 2 * p, 2 * p + 1
            q = q_ref[0, pl.ds(t0, L), lanes]
            k = k_ref[0, pl.ds(t0, L), lanes]
            v = v_ref[0, pl.ds(t0, L), lanes]
            zero = jnp.zeros_like(q)
            qq = jnp.concatenate([jnp.where(head0_lane, q, zero), jnp.where(head0_lane, zero, q)], axis=0)
            qk = _dot_nt(k, qq)
            cst = cst_ref[p]
            nst = nst_ref[p]
            sc, rs = [], []
            for h, hh in enumerate((h0, h1)):
                d = jnp.where(causal_t, cols[:, hh:hh + 1] - mm_all[hh:hh + 1, :], NEG_INF)
                s_h = qk[:, h * L:(h + 1) * L] * jnp.exp(d)
                sc.append(s_h.astype(BF16))
                rs.append(jnp.sum(s_h, axis=0, keepdims=True))
            ws_l = jnp.where(head0_lane, cols[:, nheads + h0:nheads + h0 + 1], cols[:, nheads + h1:nheads + h1 + 1])
            kws = k.astype(F32) * ws_l
            u = _dot_tn(v, jnp.concatenate(sc + [kws.astype(BF16)], axis=1))
            intra = jnp.where(head0_row, u[:, :L], u[:, L:2 * L])
            inter = _dot_nt(cst.astype(BF16), q)
            qn = _dot_nt(nst.astype(BF16), q)
            iw_l = jnp.where(head0_row, iw_all[h0:h0 + 1, :], iw_all[h1:h1 + 1, :])
            den0 = iw_all[h0:h0 + 1, :] * qn[0:1, :] + rs[0]
            den1 = iw_all[h1:h1 + 1, :] * qn[1:2, :] + rs[1]
            lim = jnp.maximum(jnp.abs(jnp.where(head0_row, den0, den1)),
                              jnp.where(head0_row, en_all[h0:h0 + 1, :], en_all[h1:h1 + 1, :]))
            h_ref[0, pl.ds(t0, L), lanes] = ((iw_l * inter + intra) / lim).T

            cw0 = iw_all[h0:h0 + 1, L - 1:L]
            cw1 = iw_all[h1:h1 + 1, L - 1:L]
            cst_ref[p] = jnp.where(head0_row, cw0, cw1) * cst + jnp.where(blockdiag, u[:, 2 * L:], 0.0)
            ksum = jnp.sum(kws, axis=0, keepdims=True)
            own = (row8 == 0) == (lane8 < HEAD_DIM)
            nst_ref[p] = jnp.where((row8 < 2) & own, jnp.where(row8 == 0, cw0, cw1) * nst + ksum, 0.0)
        return carry

    lax.fori_loop(0, nchunks, chunk, 0, unroll=4)

    @pl.when(pl.program_id(1) == pl.num_programs(1) - 1)
    def _():
        for p in range(npairs):
            for h in range(2):
                sl = slice(h * HEAD_DIM, (h + 1) * HEAD_DIM)
                c_ref[0, 2 * p + h] = cst_ref[p, sl, sl]
                n_ref[0, 2 * p + h:2 * p + h + 1, :] = nst_ref[p, h:h + 1, sl]
        m_ref[0] = mst_ref[...]


def _mlstm(q, k, v, gates, sq, skn, svn, ckt, cvt, layer, *, ts):
    b, s, wb = q.shape
    nheads = wb // HEAD_DIM
    nj = s // ts
    dec_b, nh_a, dh = sq.shape
    per_step = dec_b // (b * nj)
    assert per_step * b * nj == dec_b
    spec = pl.BlockSpec((1, ts, wb), lambda i, j: (i, j, 0))
    srow = pl.BlockSpec((per_step, nh_a, dh), lambda i, j: (i * nj + j, 0, 0))
    cache = pl.BlockSpec((None, per_step, nh_a, dh, ckt.shape[-1]), lambda i, j: (layer, i * nj + j, 0, 0, 0))
    out_shape = (jax.ShapeDtypeStruct((b, s, wb), F32),
                 jax.ShapeDtypeStruct((b, nheads, HEAD_DIM, HEAD_DIM), F32),
                 jax.ShapeDtypeStruct((b, nheads, HEAD_DIM), F32),
                 jax.ShapeDtypeStruct((b, nheads, LANES), F32),
                 jax.ShapeDtypeStruct((dec_b, nh_a, dh), F32))
    out_specs = (spec,
                 pl.BlockSpec((1, nheads, HEAD_DIM, HEAD_DIM), lambda i, j: (i, 0, 0, 0)),
                 pl.BlockSpec((1, nheads, HEAD_DIM), lambda i, j: (i, 0, 0)),
                 pl.BlockSpec((1, nheads, LANES), lambda i, j: (i, 0, 0)),
                 srow)
    return pl.pallas_call(
        functools.partial(_mlstm_body, nchunks=ts // CHUNK, nheads=nheads),
        grid=(b, nj),
        in_specs=[spec, spec, spec, pl.BlockSpec((1, 2 * nheads, ts), lambda i, j: (i, 0, j)),
                  srow, srow, srow, cache, cache],
        out_specs=out_specs,
        out_shape=out_shape,
        scratch_shapes=[pltpu.VMEM((nheads // 2, LANES, LANES), F32), pltpu.VMEM((nheads // 2, 8, LANES), F32),
                        pltpu.VMEM((nheads, LANES), F32), pltpu.VMEM((5, nheads, ts), F32)],
        compiler_params=_cparams(2),
        name="mlstm",
    )(q, k, v, gates, sq, skn, svn, ckt, cvt)


def _post_body(*refs, prompt, seq_tiles, ck):
    (x_ref, oa_ref, hb_ref, og_ref, p_ref, ga_ref, gb_ref, wo_ref, g_ref, wup_ref, cw_ref, cb_ref, wdn_ref,
     gp_ref, wpg_ref, wpp_ref) = refs[:16]
    if prompt:
        o_ref, conv_ref, y_ref, carry_ref = refs[16:]
    else:
        p0_ref, p1_ref, o_ref, conv_ref, y_ref = refs[16:]
    tm = x_ref.shape[0]
    dff = wdn_ref.shape[0]
    wa = oa_ref.shape[1]
    ya = _rms(oa_ref[...], ga_ref[...]).astype(BF16)
    yb = (og_ref[...] * _rms(hb_ref[...], gb_ref[...])).astype(BF16)
    x = x_ref[...] + _dot(ya, wo_ref[0:wa, :]) + _dot(yb, wo_ref[wa:, :])
    h = _rms(x, g_ref[...]).astype(BF16)
    if prompt:
        @pl.when(pl.program_id(0) % seq_tiles == 0)
        def _():
            carry_ref[...] = jnp.zeros_like(carry_ref)
        row = lax.broadcasted_iota(jnp.int32, (tm, ck), 0)
    for c in range(dff // ck):
        cs = slice(c * ck, (c + 1) * ck)
        gate = _dot(h, wup_ref[:, c * ck:(c + 1) * ck])
        val = _dot(h, wup_ref[:, dff + c * ck:dff + (c + 1) * ck])
        if prompt:
            b0 = carry_ref[0:1, cs]
            b1 = carry_ref[1:2, cs]
            g1 = jnp.where(row == 0, b1, pltpu.roll(gate, 1, 0))
            g2 = jnp.where(row == 0, b0, jnp.where(row == 1, b1, pltpu.roll(gate, 2, 0)))
            carry_ref[:, cs] = gate[tm - 2:tm, :]
        else:
            g2 = p0_ref[:, cs]
            g1 = p1_ref[:, cs]
            conv_ref[:, cs] = gate
        gc = cb_ref[:, cs] + (cw_ref[0:1, cs] * g2 + cw_ref[1:2, cs] * g1 + cw_ref[2:3, cs] * gate)
        y_ref[:, cs] = (_gelu_tanh(gc) * val).astype(BF16)
    if prompt:
        conv_ref[0] = carry_ref[...]
    x = x + _dot(y_ref[...], wdn_ref[...])
    gate = jax.nn.sigmoid(_dot(_rms(x, gp_ref[...]).astype(BF16), wpg_ref[...]))
    o_ref[...] = x + gate * _dot(p_ref[...].astype(BF16), wpp_ref[...])


def _post(x, oa, hb, og, p, layer, params, prev=None, *, tm, seq_tiles=1, batch=1):
    t, d = x.shape
    dff = params[7].shape[1]
    prompt = prev is None
    tok = lambda w_: pl.BlockSpec((tm, w_), lambda i: (i, 0))
    in_specs = [tok(d), tok(oa.shape[1]), tok(hb.shape[1]), tok(og.shape[1]),
                pl.BlockSpec((None, tm, p.shape[2]), lambda i: (layer, i, 0))]
    in_specs += [_layer_spec(a, layer) for a in params]
    args = [x, oa, hb, og, p, *params]
    scratch = [pltpu.VMEM((tm, dff), BF16)]
    if prompt:
        conv_shape = jax.ShapeDtypeStruct((batch, 2, dff), F32)
        conv_spec = pl.BlockSpec((1, 2, dff), lambda i: (i // seq_tiles, 0, 0))
        scratch.append(pltpu.VMEM((2, dff), F32))
    else:
        in_specs += [tok(dff), tok(dff)]
        args += list(prev)
        conv_shape = jax.ShapeDtypeStruct((t, dff), F32)
        conv_spec = tok(dff)
    return pl.pallas_call(
        functools.partial(_post_body, prompt=prompt, seq_tiles=seq_tiles, ck=FFN_CHUNK),
        grid=(t // tm,),
        in_specs=in_specs,
        out_specs=(tok(d), conv_spec),
        out_shape=(jax.ShapeDtypeStruct((t, d), F32), conv_shape),
        scratch_shapes=scratch,
        compiler_params=_cparams(1),
        name="post_mixer",
    )(*args)


def _sample_attn_rows(q_ref, kn_ref, vn_ref, kt_ref, vt_ref, o_ref):
    nrows, nh, _, wbuf = kt_ref.shape
    r = lax.broadcasted_iota(jnp.int32, (HEAD_DIM, HEAD_DIM), 0)
    c = lax.broadcasted_iota(jnp.int32, (HEAD_DIM, HEAD_DIM), 1)
    diag = r == c
    dist = wbuf - lax.broadcasted_iota(jnp.int32, (1, wbuf), 1)
    for b in range(nrows):
        q = q_ref[b]
        kn = kn_ref[b]
        vn = vn_ref[b]
        s_new = jnp.sum(q * kn, axis=-1, keepdims=True)
        rows = []
        for h in range(nh):
            q_col = jnp.sum(jnp.where(diag, q[h:h + 1, :], 0.0), axis=1, keepdims=True)
            rows.append(jnp.sum(kt_ref[b, h] * q_col, axis=0, keepdims=True))
        s = jnp.concatenate(rows, axis=0)
        stats = []
        for dil in DILATIONS:
            mask = ((dist & (dil - 1)) == 0) & (dist <= BAND * dil)
            sg = jnp.where(mask, s, NEG_INF)
            m = jnp.maximum(jnp.max(sg, axis=1, keepdims=True), s_new)
            e = jnp.exp(sg - m)
            e_new = jnp.exp(s_new - m)
            stats.append((m, e, e_new, jnp.sum(e, axis=1, keepdims=True) + e_new))
        mm = jnp.maximum(jnp.maximum(stats[0][0], stats[1][0]), stats[2][0])
        cs = [jnp.exp(m - mm) for m, _, _, _ in stats]
        wgt = cs[0] * stats[0][1] + cs[1] * stats[1][1] + cs[2] * stats[2][1]
        w_new = cs[0] * stats[0][2] + cs[1] * stats[1][2] + cs[2] * stats[2][2]
        den = cs[0] * stats[0][3] + cs[1] * stats[1][3] + cs[2] * stats[2][3]
        outs = []
        for h in range(nh):
            o_col = jnp.sum(vt_ref[b, h] * wgt[h:h + 1, :], axis=1, keepdims=True)
            outs.append(jnp.sum(jnp.where(diag, o_col, 0.0), axis=0, keepdims=True))
        o_ref[b] = (jnp.concatenate(outs, axis=0) + w_new * vn) / den


def _smlstm_body(q_ref, k_ref, v_ref, ig_ref, lf_ref, c_ref, n_ref, m_ref, *refs):
    h_ref, co_ref, no_ref, mo_ref = refs[-4:]
    r = lax.broadcasted_iota(jnp.int32, (HEAD_DIM, HEAD_DIM), 0)
    l = lax.broadcasted_iota(jnp.int32, (HEAD_DIM, HEAD_DIM), 1)
    nh = q_ref.shape[1]
    for b in range(q_ref.shape[0]):
        q, k, v = q_ref[b], k_ref[b], v_ref[b]
        ig, lf, m = ig_ref[b], lf_ref[b], m_ref[b]
        n = n_ref[b]
        inter = lf + m
        m_t = jnp.maximum(inter, ig)
        dw = jnp.exp(ig - m_t)
        iw = jnp.exp(inter - m_t)
        sc = jnp.sum(q * k, axis=-1, keepdims=True) * dw
        cq = []
        for h in range(nh):
            c_h = c_ref[b, h]
            q_h = jnp.broadcast_to(q[h:h + 1, :], (8, HEAD_DIM)).astype(BF16)
            cq.append(_dot_nt(q_h, c_h.astype(BF16))[0:1, :])
            v_col = jnp.sum(jnp.where(r == l, v[h:h + 1, :], 0.0), axis=1, keepdims=True)
            co_ref[b, h] = iw[h:h + 1, :] * c_h + (dw[h:h + 1, :] * v_col) * k[h:h + 1, :]
        num = iw * jnp.concatenate(cq, axis=0) + sc * v
        den = iw * jnp.sum(n * q, axis=-1, keepdims=True) + sc
        h_ref[b] = num / jnp.maximum(jnp.abs(den), jnp.exp(-m_t))
        no_ref[b] = iw * n + dw * k
        mo_ref[b] = m_t


def _sample_mlstm(q, k, v, ig, lf, c, n, m, layer, c_acc, *, rows):
    b, nh, dh = q.shape
    depth = c.shape[0]
    row = pl.BlockSpec((rows, nh, dh), lambda i: (i, 0, 0))
    col = pl.BlockSpec((rows, nh, 1), lambda i: (i, 0, 0))
    lrow = pl.BlockSpec((None, rows, nh, dh), lambda i: (layer, i, 0, 0))
    lcol = pl.BlockSpec((None, rows, nh, 1), lambda i: (layer, i, 0, 0))
    lcst = pl.BlockSpec((None, rows, nh, dh, dh), lambda i: (layer, i, 0, 0, 0))
    args = [q, k, v, ig, lf, c, n, m]
    in_specs = [row, row, row, col, col, lcst, lrow, lcol]
    kwargs = {"input_output_aliases": {len(args): 1}}
    args.append(c_acc)
    in_specs.append(pl.BlockSpec(memory_space=pl.ANY))
    return pl.pallas_call(
        _smlstm_body,
        grid=(b // rows,),
        in_specs=in_specs,
        out_specs=(row, lcst, row, col),
        out_shape=(jax.ShapeDtypeStruct((b, nh, dh), F32), jax.ShapeDtypeStruct((depth, b, nh, dh, dh), F32),
                   jax.ShapeDtypeStruct((b, nh, dh), F32), jax.ShapeDtypeStruct((b, nh, 1), F32)),
        compiler_params=_cparams(1),
        name="sample_mlstm",
        **kwargs,
    )(*args)


def _rope_tables(pos):
    half = ROT_DIM // 2
    inv = ROPE_THETA ** (-jnp.arange(half, dtype=F32) / half)
    ang = pos.astype(F32)[:, None] * inv[None, :]
    cos, sin = jnp.cos(ang), jnp.sin(ang)
    n = pos.shape[0]
    rest = HEAD_DIM - ROT_DIM
    c = jnp.concatenate([cos, cos, jnp.ones((n, rest), F32)], axis=1)
    sa = jnp.concatenate([-sin, jnp.zeros((n, half + rest), F32)], axis=1)
    sb = jnp.concatenate([jnp.zeros((n, half), F32), sin, jnp.zeros((n, rest), F32)], axis=1)
    rep = LANES // HEAD_DIM
    return tuple(jnp.concatenate([t] * rep, axis=1) for t in (c, sa, sb))


def _rope_angles_t(pos):
    half = ROT_DIM // 2
    inv = ROPE_THETA ** (-jnp.arange(half, dtype=F32) / half)
    ang = pos.astype(F32)[:, None] * inv[None, :]
    return jnp.cos(ang).T, jnp.sin(ang).T


def kernel(x_prompt, x_sample, cache_win_k, cache_win_v, state_C, state_n, state_m, state_conv, p_prompt, p_sample,
           norm_mix, w_in, q_norm, k_norm, b_gates, out_norm_a, out_norm_b, w_out, norm_ffn, w_up, conv_w, conv_b,
           w_down, norm_ple, w_ple_gate, w_ple_proj):
    batch, seq, d = x_prompt.shape
    dec_batch, dec_seq, _ = x_sample.shape
    depth = w_in.shape[0]
    nh_a = cache_win_k.shape[3]
    nh_b = state_C.shape[2]
    wa, wb = nh_a * HEAD_DIM, nh_b * HEAD_DIM
    wbuf = cache_win_k.shape[2]
    win_max = BAND * DILATIONS[-1]
    keep = min(win_max, seq)
    assert dec_seq == 1 and wbuf == win_max and seq % win_max == 0
    assert w_in.shape[2] == 3 * wa + 4 * wb + 2 * nh_b and wa % LANES == 0 and wb % LANES == 0

    tm = min(TOKEN_TILE, seq)
    seq_tiles = seq // tm
    tp, ts = batch * seq, dec_batch * dec_seq

    cut = 3 * wa + 4 * wb
    w_in_t = jnp.transpose(w_in, (0, 2, 1))
    w_main = w_in_t[:, :cut, :].astype(BF16)
    w_gate = w_in_t[:, cut + 2 * nh_b - LANES:, :].astype(BF16)
    bg = jnp.concatenate([jnp.zeros((depth, GATE_LANE0), F32), b_gates], axis=1)[:, None, :]
    qn = jnp.tile(q_norm, (1, nh_a))[:, None, :]
    kn = jnp.tile(k_norm, (1, nh_a))[:, None, :]
    rows = lambda a: a[:, None, :]
    cols = lambda a: jnp.broadcast_to(a[:, 0, :, None], (depth, wa, LANES))
    inproj_params_s = (rows(norm_mix), w_main, w_gate, qn, kn, bg)
    inproj_params_p = (rows(norm_mix), w_main, w_gate, cols(qn), cols(kn), bg)
    post_params = (rows(out_norm_a), rows(out_norm_b), w_out.astype(BF16), rows(norm_ffn), w_up.astype(BF16),
                   conv_w, rows(conv_b), w_down.astype(BF16), rows(norm_ple), w_ple_gate.astype(BF16),
                   w_ple_proj.astype(BF16))
    pp = p_prompt.reshape(depth, tp, -1)
    ps = p_sample.reshape(depth, ts, -1)
    head_of_lane = jnp.arange(wa) // HEAD_DIM
    ind = (head_of_lane[:, None] == jnp.arange(LANES)[None, :]).astype(BF16)
    indt = ind.T
    tab_p = _rope_angles_t(jnp.arange(seq, dtype=jnp.int32))
    tab_s = tuple(jnp.broadcast_to(t, (ts, LANES))
                  for t in _rope_tables(PAST_LEN + jnp.arange(dec_seq, dtype=jnp.int32)))

    ckt = jnp.transpose(cache_win_k, (0, 1, 3, 4, 2))
    cvt = jnp.transpose(cache_win_v, (0, 1, 3, 4, 2))

    xp = x_prompt.reshape(tp, d)
    xs = x_sample.reshape(ts, d)
    outs = [[] for _ in range(12)]
    wk_acc = jnp.zeros((depth, batch, wa, keep), F32)
    wv_acc = jnp.zeros_like(wk_acc)
    cs_acc = jnp.zeros_like(state_C)
    state_m4 = state_m[..., None]
    for i in range(depth):
        q_rm, wk_acc, wv_acc, qbp, kbp, vbp, ogp, gt, k_rm, v_rm = _inproj(
            xp, i, *inproj_params_p, tab_p, (),
            tm=tm, seq_tiles=seq_tiles, batch=batch, mdtype=BF16, transpose_gates=True,
            window=(keep, depth, wk_acc, wv_acc))
        qa, ka, va, qb, kb, vb, og, ga = _inproj(
            xs, i, *inproj_params_s, tab_s, (ind, indt),
            tm=ts, seq_tiles=1, batch=dec_batch, mdtype=F32, transpose_gates=False)
        r3 = lambda a: a.reshape(batch, seq, a.shape[-1])
        hd = lambda a: a.reshape(dec_batch, -1, HEAD_DIM)
        oap = _attention(q_rm, k_rm, v_rm)
        hbp, c_new, n_new, m_new, oa = _mlstm(r3(qbp), r3(kbp), r3(vbp), gt, hd(qa), hd(ka), hd(va), ckt, cvt, i,
                                              ts=min(seq, MLSTM_TILE))
        xp, conv_new = _post(xp, oap.reshape(tp, wa), hbp.reshape(tp, wb), ogp, pp, i, post_params,
                             tm=tm, seq_tiles=seq_tiles, batch=batch)
        outs[2].append(c_new)
        outs[3].append(n_new)
        outs[4].append(m_new[..., 0])
        outs[5].append(conv_new)
        ig = ga[:, GATE_LANE0:GATE_LANE0 + nh_b, None]
        lf = ga[:, GATE_LANE0 + nh_b:, None]
        hb, cs_acc, n_new, m_new = _sample_mlstm(hd(qb), hd(kb), hd(vb), ig, lf, state_C, state_n, state_m4, i,
                                                 cs_acc, rows=SAMPLE_ROWS)
        xs, gate_new = _post(xs, oa.reshape(ts, wa), hb.reshape(ts, wb), og, ps, i, post_params,
                             prev=(state_conv[i][:, 0], state_conv[i][:, 1]), tm=ts)
        outs[6].append(ka.reshape(dec_batch, dec_seq, nh_a, HEAD_DIM))
        outs[7].append(va.reshape(dec_batch, dec_seq, nh_a, HEAD_DIM))
        outs[9].append(n_new)
        outs[10].append(m_new[..., 0])
        outs[11].append(jnp.stack([state_conv[i][:, 1], gate_new], axis=1))
    def window_out(acc):
        return jnp.transpose(acc.reshape(depth, batch, nh_a, HEAD_DIM, keep), (0, 1, 4, 2, 3))

    st = jnp.stack
    return ((xp.reshape(batch, seq, d), xs.reshape(dec_batch, dec_seq, d), window_out(wk_acc), window_out(wv_acc))
            + tuple(st(o) for o in outs[2:8]) + (cs_acc,) + tuple(st(o) for o in outs[9:]))
```

```python
import functools

import jax
import jax.numpy as jnp
from jax import lax
from jax.experimental import pallas as pl
from jax.experimental.pallas import tpu as pltpu

F32 = jnp.float32
BF16 = jnp.bfloat16

HEAD_DIM = 64
ROT_DIM = HEAD_DIM // 4
ROPE_THETA = 500000.0
RMS_EPS = 1e-6
NEG_INF = -1e30
PAST_LEN = 8192
DILATIONS = (1, 4, 16)
NRES = 4
BAND = 128
ATTN_GROUP = 32
CHUNK = 128
FFN_CHUNK = 256
TOKEN_TILE = 512
MLSTM_TILE = 1024
SAMPLE_ROWS = 8
LANES = 128
GATE_LANE0 = LANES - 16
VMEM_LIMIT = 56 * 1024 * 1024


def _cparams(n_axes):
    return pltpu.CompilerParams(dimension_semantics=("arbitrary",) * n_axes,
                                vmem_limit_bytes=VMEM_LIMIT)


def _const_spec(shape):
    return pl.BlockSpec(shape, lambda *_: (0,) * len(shape))


def _layer_spec(stack, layer):
    nd = stack.ndim
    return pl.BlockSpec((None,) + stack.shape[1:], lambda *_: (layer,) + (0,) * (nd - 1),
                        pipeline_mode=pl.Buffered(1))


def _rms(x, g):
    return (x * lax.rsqrt(jnp.mean(x * x, axis=-1, keepdims=True) + RMS_EPS)) * g


def _dot(a, b):
    return jnp.dot(a, b, preferred_element_type=F32)


def _dot_nt(a, b):
    return lax.dot_general(a, b, (((1,), (1,)), ((), ())), preferred_element_type=F32)


def _dot_tn(a, b):
    return lax.dot_general(a, b, (((0,), (0,)), ((), ())), preferred_element_type=F32)


def _split_dot(a, b):
    hi = a.astype(BF16)
    lo = (a - hi.astype(F32)).astype(BF16)
    return _dot(hi, b) + _dot(lo, b)


def _log_sigmoid(x):
    return jnp.minimum(x, 0.0) - jnp.log1p(jnp.exp(-jnp.abs(x)))


def _gelu_tanh(x):
    c = 0.7978845608028654
    return x * (0.5 * (1.0 + jnp.tanh(c * (x + 0.044715 * (x * x * x)))))


def _inproj_body(x_ref, g_ref, w_ref, wg_ref, qn_ref, kn_ref, bg_ref, *refs, wa, transpose_gates, n_acc=0):
    h = _rms(x_ref[...], g_ref[...]).astype(BF16)
    proj = lambda lo, hi: _dot_nt(h, w_ref[lo:hi, :])
    reps = wa // LANES
    half = ROT_DIM // 2
    if transpose_gates:
        cos_ref, sin_ref = refs[:2]
        qa_ref, ka_ref, va_ref, qb_ref, kb_ref, vb_ref, og_ref, gt_ref, *rest = refs[2 + n_acc:]
        tiles = x_ref.shape[0] // LANES

        def head_norm_rope(z, gain_ref):
            zt = z.T
            cos, sin = cos_ref[...], sin_ref[...]
            heads = []
            for hh in range(wa // HEAD_DIM):
                x = zt[hh * HEAD_DIM:(hh + 1) * HEAD_DIM, :]
                gain = jnp.concatenate([gain_ref[hh * HEAD_DIM:(hh + 1) * HEAD_DIM, :]] * tiles, axis=1)
                xn = (x * lax.rsqrt(jnp.mean(x * x, axis=0, keepdims=True) + RMS_EPS)) * gain
                x1, x2 = xn[0:half], xn[half:2 * half]
                heads.append(jnp.concatenate([x1 * cos - x2 * sin, x2 * cos + x1 * sin, xn[2 * half:]], axis=0))
            zt = jnp.concatenate(heads, axis=0)
            return zt, zt.T
    else:
        cos_ref, sa_ref, sb_ref, ind_ref, indt_ref = refs[:5]
        qa_ref, ka_ref, va_ref, qb_ref, kb_ref, vb_ref, og_ref, gt_ref, *rest = refs[5:]
        cos = jnp.concatenate([cos_ref[...]] * reps, axis=1)
        sa = jnp.concatenate([sa_ref[...]] * reps, axis=1)
        sb = jnp.concatenate([sb_ref[...]] * reps, axis=1)
        ind = ind_ref[...]
        indt = indt_ref[...]

        def head_norm_rope(z, gain_ref):
            ss = _split_dot(z * z, ind)
            inv = lax.rsqrt(ss * (1.0 / HEAD_DIM) + RMS_EPS)
            zn = (z * _split_dot(inv, indt)) * gain_ref[...]
            return None, zn * cos + pltpu.roll(zn, wa - half, 1) * sa + pltpu.roll(zn, half, 1) * sb

    qa = head_norm_rope(proj(0, wa), qn_ref)[1] * (HEAD_DIM ** -0.5)
    kt, ka = head_norm_rope(proj(wa, 2 * wa), kn_ref)
    va = proj(2 * wa, 3 * wa)
    if transpose_gates:
        ka_ref[...] = kt
        va_ref[...] = va.T
        krm_ref, vrm_ref, zs_ref = rest
        rows = zs_ref.shape[1] // NRES
        for val, dst in ((qa, qa_ref), (ka, krm_ref), (va, vrm_ref)):
            for c in range(reps):
                zs_ref[c] = val[:, c * LANES:(c + 1) * LANES]
            for r in range(NRES):
                for c in range(reps):
                    dst[0, r, :, c * LANES:(c + 1) * LANES] = zs_ref[c, pl.ds(r, rows, stride=NRES), :]
    else:
        qa_ref[...] = qa
        ka_ref[...] = ka
        va_ref[...] = va
    wb = (w_ref.shape[0] - 3 * wa) // 4
    o = 3 * wa
    qb_ref[...] = proj(o, o + wb).astype(qb_ref.dtype)
    kb_ref[...] = (proj(o + wb, o + 2 * wb) * (HEAD_DIM ** -0.5)).astype(kb_ref.dtype)
    vb_ref[...] = proj(o + 2 * wb, o + 3 * wb).astype(vb_ref.dtype)
    og_ref[...] = jax.nn.sigmoid(proj(o + 3 * wb, o + 4 * wb))

    v = _dot_nt(h, wg_ref[...]) + bg_ref[...]
    lane = lax.broadcasted_iota(jnp.int32, v.shape, 1)
    ga = jnp.where(lane >= GATE_LANE0 + 8, _log_sigmoid(v), v)
    if transpose_gates:
        gt_ref[0] = ga.T[GATE_LANE0:, :]
    else:
        gt_ref[...] = ga


def _inproj(x, layer, gain, w_main, w_gate, qn, kn, bg, tables, consts, *, tm, seq_tiles, batch, mdtype,
            transpose_gates, window=None):
    t, d = x.shape
    wa = qn.shape[1] if transpose_gates else qn.shape[2]
    wb = (w_main.shape[1] - 3 * wa) // 4
    nt = t // tm
    tok = lambda w: pl.BlockSpec((tm, w), lambda i: (i, 0))
    sds = jax.ShapeDtypeStruct
    args = [x, gain, w_main, w_gate, qn, kn, bg, *tables, *consts]
    in_specs = [tok(d)] + [_layer_spec(a, layer) for a in (gain, w_main, w_gate, qn, kn, bg)]
    tail_shape = [sds((t, wb), mdtype), sds((t, wb), mdtype), sds((t, wb), mdtype), sds((t, wb), F32)]
    tail_specs = [tok(wb), tok(wb), tok(wb), tok(wb)]
    kwargs = {}
    if transpose_gates:
        seq = seq_tiles * tm
        keep, depth, k_acc, v_acc = window
        first = (seq - keep) // tm
        in_specs += [pl.BlockSpec((tables[0].shape[0], tm), lambda i: (0, i % seq_tiles))] * len(tables)
        rm_shape = sds((batch, NRES, seq // NRES, wa), F32)
        rm_spec = pl.BlockSpec((1, NRES, tm // NRES, wa), lambda i: (i // seq_tiles, 0, i % seq_tiles, 0))
        win_shape = sds((depth, batch, wa, keep), F32)
        win_spec = pl.BlockSpec((None, None, wa, tm),
                                lambda i: (layer, i // seq_tiles, 0, jnp.maximum(i % seq_tiles - first, 0)))
        gt_shape = sds((batch, 16, seq), F32)
        gt_spec = pl.BlockSpec((1, 16, tm), lambda i: (i // seq_tiles, 0, i % seq_tiles))
        out_shape = [rm_shape, win_shape, win_shape] + tail_shape + [gt_shape, rm_shape, rm_shape]
        out_specs = [rm_spec, win_spec, win_spec] + tail_specs + [gt_spec, rm_spec, rm_spec]
        scratch = [pltpu.VMEM((wa // LANES, tm, LANES), F32)]
        kwargs["input_output_aliases"] = {len(args): 1, len(args) + 1: 2}
        args += [k_acc, v_acc]
        in_specs += [pl.BlockSpec(memory_space=pl.ANY)] * 2
        body = functools.partial(_inproj_body, wa=wa, transpose_gates=True, n_acc=2)
    else:
        in_specs += [pl.BlockSpec((tm, LANES), lambda i: (i % seq_tiles, 0))] * len(tables)
        in_specs += [_const_spec(c.shape) for c in consts]
        out_shape = [sds((t, wa), F32)] * 3 + tail_shape + [sds((t, LANES), F32)]
        out_specs = [tok(wa)] * 3 + tail_specs + [tok(LANES)]
        scratch = []
        body = functools.partial(_inproj_body, wa=wa, transpose_gates=False)
    return pl.pallas_call(
        body,
        grid=(nt,),
        in_specs=in_specs,
        out_specs=tuple(out_specs),
        out_shape=tuple(out_shape),
        scratch_shapes=scratch,
        compiler_params=_cparams(1),
        name="inproj",
        **kwargs,
    )(*args)


def _attn_body(q_ref, k_ref, v_ref, o_ref, acc_ref, m0_ref, m1_ref, l0_ref, l1_ref, *, seq):
    head0 = lax.broadcasted_iota(jnp.int32, (BAND, LANES), 1) < HEAD_DIM
    qrow = lax.broadcasted_iota(jnp.int32, (2 * BAND, 2 * BAND), 0) % BAND
    kcol = lax.broadcasted_iota(jnp.int32, (2 * BAND, 2 * BAND), 1)
    in_cur = kcol >= BAND
    qrow1 = lax.broadcasted_iota(jnp.int32, (2 * BAND, BAND), 0) % BAND
    kcol1 = lax.broadcasted_iota(jnp.int32, (2 * BAND, BAND), 1)

    for pi, dil in enumerate(DILATIONS):
        nb = seq // (dil * BAND)
        first, last = pi == 0, pi == len(DILATIONS) - 1
        grp = min(ATTN_GROUP, nb)
        nres = ATTN_GROUP // grp
        ngrp = nb // grp
        u = max(NRES // dil, 1)
        sl = BAND // u
        stride = max(dil // NRES, 1)
        assert not last or u == 1
        pos = lambda i, u=u, sl=sl: u * (i % sl) + i // sl
        dist = (BAND + pos(qrow)) - (BAND * (kcol // BAND) + pos(kcol % BAND))
        band = (dist >= 0) & (dist <= BAND)
        tri = pos(kcol1) <= pos(qrow1)

        def group(idx, carry, dil=dil, first=first, last=last, grp=grp, nres=nres, ngrp=ngrp, u=u, sl=sl,
                  stride=stride, band=band, tri=tri):
            rr, gi = (idx // ngrp, idx % ngrp) if ngrp > 1 else (idx, 0)

            def rows_of(r_d, blk, n):
                if dil <= NRES:
                    a = blk * sl
                    a = a if isinstance(a, int) else pl.multiple_of(a, 8)
                    return [(dil * c + r_d, pl.ds(a, n * sl)) for c in range(u)]
                return [(r_d % NRES, pl.ds(stride * BAND * blk + r_d // NRES, n * BAND, stride=stride))]

            def slabs(ref, r_d, blk, n):
                parts = [ref[i0, rows, :] for i0, rows in rows_of(r_d, blk, n)]
                return [jnp.concatenate([p[j * sl:(j + 1) * sl] for p in parts], axis=0) for j in range(n)]

            tasks = []
            for ri in range(nres):
                r_d = rr * nres + ri
                a0 = gi * grp
                kb = [x.astype(BF16) for x in slabs(k_ref.at[0], r_d, a0, grp)]
                vb = [x.astype(BF16) for x in slabs(v_ref.at[0], r_d, a0, grp)]
                if ngrp > 1:
                    ap = jnp.maximum(a0 - 1, 0)
                    kb.insert(0, slabs(k_ref.at[0], r_d, ap, 1)[0].astype(BF16))
                    vb.insert(0, slabs(v_ref.at[0], r_d, ap, 1)[0].astype(BF16))
                qs = slabs(q_ref.at[0], r_d, a0, grp)
                if not first:
                    olds = [slabs(ref, r_d, a0, grp) for ref in (m0_ref, m1_ref, l0_ref, l1_ref, acc_ref)]
                for j in range(grp):
                    jj = j + (1 if ngrp > 1 else 0)
                    old = None
                    if not first:
                        old = (jnp.concatenate([olds[0][j], olds[1][j]], axis=0),
                               jnp.concatenate([olds[2][j], olds[3][j]], axis=0), olds[4][j])
                    if jj == 0:
                        kk, vv, mask = kb[0], vb[0], tri
                    else:
                        kk = jnp.concatenate([kb[jj - 1], kb[jj]], axis=0)
                        vv = jnp.concatenate([vb[jj - 1], vb[jj]], axis=0)
                        mask = band & (in_cur | (gi > 0)) if (j == 0 and ngrp > 1) else band
                    tasks.append(((r_d, a0 + j), qs[j], kk, vv, mask, old))
            results = []
            for rows_j, q, kk, vv, mask, old in tasks:
                qq = jnp.concatenate([jnp.where(head0, q, 0.0), jnp.where(head0, 0.0, q)], axis=0).astype(BF16)
                s = jnp.where(mask, _dot_nt(qq, kk), NEG_INF)
                rm = jnp.max(s, axis=1, keepdims=True)
                if first:
                    mn = jnp.broadcast_to(rm, (2 * BAND, LANES))
                else:
                    mn = jnp.maximum(old[0], rm)
                p = jnp.exp((s - jnp.concatenate([mn] * (s.shape[1] // LANES), axis=1)).astype(BF16))
                pv = _dot(p, jnp.concatenate([vv, jnp.ones_like(vv)], axis=1))
                ls = pv[:, LANES:]
                acc = jnp.where(head0, pv[:BAND, :LANES], pv[BAND:, :LANES])
                if first:
                    ln = ls
                else:
                    a = jnp.exp(old[0] - mn)
                    ln = a * old[1] + ls
                    acc = jnp.where(head0, a[:BAND], a[BAND:]) * old[2] + acc
                results.append((rows_j, mn, ln, acc))
            for (r_d, blk), mn, ln, acc in results:
                if last:
                    o_ref[0, pl.ds(dil * BAND * blk + r_d, BAND, stride=dil), :] = (
                        acc / jnp.where(head0, ln[:BAND], ln[BAND:]))
                else:
                    for ref, val in ((m0_ref, mn[:BAND]), (m1_ref, mn[BAND:]), (l0_ref, ln[:BAND]),
                                     (l1_ref, ln[BAND:]), (acc_ref, acc)):
                        for c, (i0, rows) in enumerate(rows_of(r_d, blk, 1)):
                            ref[i0, rows, :] = val[c * sl:(c + 1) * sl]
            return carry

        lax.fori_loop(0, (dil * nb) // ATTN_GROUP, group, 0)


def _attention(q, k, v):
    b, _, la, wa = q.shape
    s = la * NRES
    spec = pl.BlockSpec((1, NRES, la, LANES), lambda i, j: (i, 0, 0, j))
    scratch = [pltpu.VMEM((NRES, la, LANES), F32) for _ in range(5)]
    return pl.pallas_call(
        functools.partial(_attn_body, seq=s),
        grid=(b, wa // LANES),
        in_specs=[spec, spec, spec],
        out_specs=pl.BlockSpec((1, s, LANES), lambda i, j: (i, 0, j)),
        out_shape=jax.ShapeDtypeStruct((b, s, wa), F32),
        scratch_shapes=scratch,
        compiler_params=_cparams(2),
        name="dilated_attn",
    )(q, k, v)


def _mlstm_body(q_ref, k_ref, v_ref, g_ref, sq_ref, skn_ref, svn_ref, skt_ref, svt_ref,
                h_ref, c_ref, n_ref, m_ref, so_ref, cst_ref, nst_ref, mst_ref, gs_ref, *, nchunks, nheads):
    _sample_attn_rows(sq_ref, skn_ref, svn_ref, skt_ref, svt_ref, so_ref)
    L = CHUNK
    npairs = nheads // 2
    row = lax.broadcasted_iota(jnp.int32, (L, L), 0)
    lane = lax.broadcasted_iota(jnp.int32, (L, L), 1)
    lane_h = lax.broadcasted_iota(jnp.int32, (nheads, L), 1)
    causal_t = row <= lane
    head0_lane = lane < HEAD_DIM
    head0_row = row < HEAD_DIM
    blockdiag = head0_row == head0_lane
    lane8 = lax.broadcasted_iota(jnp.int32, (8, L), 1)
    row8 = lax.broadcasted_iota(jnp.int32, (8, L), 0)
    pad = jnp.zeros((L - 2 * nheads, L), F32)

    @pl.when(pl.program_id(1) == 0)
    def _():
        cst_ref[...] = jnp.zeros_like(cst_ref)
        nst_ref[...] = jnp.zeros_like(nst_ref)
        mst_ref[...] = jnp.zeros_like(mst_ref)

    def scan_lanes(x, op, fill):
        sh = 1
        while sh < L:
            x = op(x, jnp.where(lane_h >= sh, pltpu.roll(x, sh, 1), fill))
            sh *= 2
        return x

    m_prev = mst_ref[...]
    for c in range(nchunks):
        sl = slice(c * L, (c + 1) * L)
        b_c = scan_lanes(g_ref[0, nheads:2 * nheads, sl], jnp.add, 0.0)
        g_c = g_ref[0, 0:nheads, sl] - b_c
        cm_c = scan_lanes(g_c, jnp.maximum, NEG_INF)
        mm_c = jnp.maximum(cm_c, m_prev)
        mm_last = jnp.maximum(jnp.broadcast_to(cm_c[:, L - 1:L], (nheads, L)), m_prev)
        gs_ref[0, :, sl] = g_c
        gs_ref[1, :, sl] = mm_c
        gs_ref[2, :, sl] = jnp.exp(m_prev - mm_c)
        gs_ref[3, :, sl] = jnp.exp(-(b_c + mm_c))
        gs_ref[4, :, sl] = jnp.exp(g_c - mm_last)
        m_prev = jnp.broadcast_to(b_c[:, L - 1:L], (nheads, L)) + mm_last
    mst_ref[...] = m_prev

    def chunk(c, carry):
        t0 = pl.multiple_of(c * L, L)
        g_all = gs_ref[0, :, pl.ds(t0, L)]
        mm_all = gs_ref[1, :, pl.ds(t0, L)]
        iw_all = gs_ref[2, :, pl.ds(t0, L)]
        en_all = gs_ref[3, :, pl.ds(t0, L)]
        ws_all = gs_ref[4, :, pl.ds(t0, L)]
        cols = jnp.concatenate([g_all, ws_all, pad], axis=0).T
        for p in range(npairs):
            lanes = slice(p * LANES, (p + 1) * LANES)
            h0, h1 = 2 * p, 2 * p + 1
            q = q_ref[0, pl.ds(t0, L), lanes]
            k = k_ref[0, pl.ds(t0, L), lanes]
            v = v_ref[0, pl.ds(t0, L), lanes]
            zero = jnp.zeros_like(q)
            qq = jnp.concatenate([jnp.where(head0_lane, q, zero), jnp.where(head0_lane, zero, q)], axis=0)
            qk = _dot_nt(k, qq)
            cst = cst_ref[p]
            nst = nst_ref[p]
            sc, rs = [], []
            for h, hh in enumerate((h0, h1)):
                d = jnp.where(causal_t, cols[:, hh:hh + 1] - mm_all[hh:hh + 1, :], NEG_INF)
                s_h = qk[:, h * L:(h + 1) * L] * jnp.exp(d)
                sc.append(s_h.astype(BF16))
                rs.append(jnp.sum(s_h, axis=0, keepdims=True))
            ws_l = jnp.where(head0_lane, cols[:, nheads + h0:nheads + h0 + 1], cols[:, nheads + h1:nheads + h1 + 1])
            kws = k.astype(F32) * ws_l
            u = _dot_tn(v, jnp.concatenate(sc + [kws.astype(BF16)], axis=1))
            intra = jnp.where(head0_row, u[:, :L], u[:, L:2 * L])
            inter = _dot_nt(cst.astype(BF16), q)
            qn = _dot_nt(nst.astype(BF16), q)
            iw_l = jnp.where(head0_row, iw_all[h0:h0 + 1, :], iw_all[h1:h1 + 1, :])
            den0 = iw_all[h0:h0 + 1, :] * qn[0:1, :] + rs[0]
            den1 = iw_all[h1:h1 + 1, :] * qn[1:2, :] + rs[1]
            lim = jnp.maximum(jnp.abs(jnp.where(head0_row, den0, den1)),
                              jnp.where(head0_row, en_all[h0:h0 + 1, :], en_all[h1:h1 + 1, :]))
            h_ref[0, pl.ds(t0, L), lanes] = ((iw_l * inter + intra) / lim).T

            cw0 = iw_all[h0:h0 + 1, L - 1:L]
            cw1 = iw_all[h1:h1 + 1, L - 1:L]
            cst_ref[p] = jnp.where(head0_row, cw0, cw1) * cst + jnp.where(blockdiag, u[:, 2 * L:], 0.0)
            ksum = jnp.sum(kws, axis=0, keepdims=True)
            own = (row8 == 0) == (lane8 < HEAD_DIM)
            nst_ref[p] = jnp.where((row8 < 2) & own, jnp.where(row8 == 0, cw0, cw1) * nst + ksum, 0.0)
        return carry

    lax.fori_loop(0, nchunks, chunk, 0, unroll=4)

    @pl.when(pl.program_id(1) == pl.num_programs(1) - 1)
    def _():
        for p in range(npairs):
            for h in range(2):
                sl = slice(h * HEAD_DIM, (h + 1) * HEAD_DIM)
                c_ref[0, 2 * p + h] = cst_ref[p, sl, sl]
                n_ref[0, 2 * p + h:2 * p + h + 1, :] = nst_ref[p, h:h + 1, sl]
        m_ref[0] = mst_ref[...]


def _mlstm(q, k, v, gates, sq, skn, svn, ckt, cvt, layer, *, ts):
    b, s, wb = q.shape
    nheads = wb // HEAD_DIM
    nj = s // ts
    dec_b, nh_a, dh = sq.shape
    per_step = dec_b // (b * nj)
    assert per_step * b * nj == dec_b
    spec = pl.BlockSpec((1, ts, wb), lambda i, j: (i, j, 0))
    srow = pl.BlockSpec((per_step, nh_a, dh), lambda i, j: (i * nj + j, 0, 0))
    cache = pl.BlockSpec((None, per_step, nh_a, dh, ckt.shape[-1]), lambda i, j: (layer, i * nj + j, 0, 0, 0))
    out_shape = (jax.ShapeDtypeStruct((b, s, wb), F32),
                 jax.ShapeDtypeStruct((b, nheads, HEAD_DIM, HEAD_DIM), F32),
                 jax.ShapeDtypeStruct((b, nheads, HEAD_DIM), F32),
                 jax.ShapeDtypeStruct((b, nheads, LANES), F32),
                 jax.ShapeDtypeStruct((dec_b, nh_a, dh), F32))
    out_specs = (spec,
                 pl.BlockSpec((1, nheads, HEAD_DIM, HEAD_DIM), lambda i, j: (i, 0, 0, 0)),
                 pl.BlockSpec((1, nheads, HEAD_DIM), lambda i, j: (i, 0, 0)),
                 pl.BlockSpec((1, nheads, LANES), lambda i, j: (i, 0, 0)),
                 srow)
    return pl.pallas_call(
        functools.partial(_mlstm_body, nchunks=ts // CHUNK, nheads=nheads),
        grid=(b, nj),
        in_specs=[spec, spec, spec, pl.BlockSpec((1, 2 * nheads, ts), lambda i, j: (i, 0, j)),
                  srow, srow, srow, cache, cache],
        out_specs=out_specs,
        out_shape=out_shape,
        scratch_shapes=[pltpu.VMEM((nheads // 2, LANES, LANES), F32), pltpu.VMEM((nheads // 2, 8, LANES), F32),
                        pltpu.VMEM((nheads, LANES), F32), pltpu.VMEM((5, nheads, ts), F32)],
        compiler_params=_cparams(2),
        name="mlstm",
    )(q, k, v, gates, sq, skn, svn, ckt, cvt)


def _post_body(*refs, prompt, seq_tiles, ck):
    (x_ref, oa_ref, hb_ref, og_ref, p_ref, ga_ref, gb_ref, wo_ref, g_ref, wup_ref, cw_ref, cb_ref, wdn_ref,
     gp_ref, wpg_ref, wpp_ref) = refs[:16]
    if prompt:
        o_ref, conv_ref, y_ref, carry_ref = refs[16:]
    else:
        p0_ref, p1_ref, o_ref, conv_ref, y_ref = refs[16:]
    tm = x_ref.shape[0]
    dff = wdn_ref.shape[0]
    wa = oa_ref.shape[1]
    ya = _rms(oa_ref[...], ga_ref[...]).astype(BF16)
    yb = (og_ref[...] * _rms(hb_ref[...], gb_ref[...])).astype(BF16)
    x = x_ref[...] + _dot(ya, wo_ref[0:wa, :]) + _dot(yb, wo_ref[wa:, :])
    h = _rms(x, g_ref[...]).astype(BF16)
    if prompt:
        @pl.when(pl.program_id(0) % seq_tiles == 0)
        def _():
            carry_ref[...] = jnp.zeros_like(carry_ref)
        row = lax.broadcasted_iota(jnp.int32, (tm, ck), 0)
    for c in range(dff // ck):
        cs = slice(c * ck, (c + 1) * ck)
        gate = _dot(h, wup_ref[:, c * ck:(c + 1) * ck])
        val = _dot(h, wup_ref[:, dff + c * ck:dff + (c + 1) * ck])
        if prompt:
            b0 = carry_ref[0:1, cs]
            b1 = carry_ref[1:2, cs]
            g1 = jnp.where(row == 0, b1, pltpu.roll(gate, 1, 0))
            g2 = jnp.where(row == 0, b0, jnp.where(row == 1, b1, pltpu.roll(gate, 2, 0)))
            carry_ref[:, cs] = gate[tm - 2:tm, :]
        else:
            g2 = p0_ref[:, cs]
            g1 = p1_ref[:, cs]
            conv_ref[:, cs] = gate
        gc = cb_ref[:, cs] + (cw_ref[0:1, cs] * g2 + cw_ref[1:2, cs] * g1 + cw_ref[2:3, cs] * gate)
        y_ref[:, cs] = (_gelu_tanh(gc) * val).astype(BF16)
    if prompt:
        conv_ref[0] = carry_ref[...]
    x = x + _dot(y_ref[...], wdn_ref[...])
    gate = jax.nn.sigmoid(_dot(_rms(x, gp_ref[...]).astype(BF16), wpg_ref[...]))
    o_ref[...] = x + gate * _dot(p_ref[...].astype(BF16), wpp_ref[...])


def _post(x, oa, hb, og, p, layer, params, prev=None, *, tm, seq_tiles=1, batch=1):
    t, d = x.shape
    dff = params[7].shape[1]
    prompt = prev is None
    tok = lambda w_: pl.BlockSpec((tm, w_), lambda i: (i, 0))
    in_specs = [tok(d), tok(oa.shape[1]), tok(hb.shape[1]), tok(og.shape[1]),
                pl.BlockSpec((None, tm, p.shape[2]), lambda i: (layer, i, 0))]
    in_specs += [_layer_spec(a, layer) for a in params]
    args = [x, oa, hb, og, p, *params]
    scratch = [pltpu.VMEM((tm, dff), BF16)]
    if prompt:
        conv_shape = jax.ShapeDtypeStruct((batch, 2, dff), F32)
        conv_spec = pl.BlockSpec((1, 2, dff), lambda i: (i // seq_tiles, 0, 0))
        scratch.append(pltpu.VMEM((2, dff), F32))
    else:
        in_specs += [tok(dff), tok(dff)]
        args += list(prev)
        conv_shape = jax.ShapeDtypeStruct((t, dff), F32)
        conv_spec = tok(dff)
    return pl.pallas_call(
        functools.partial(_post_body, prompt=prompt, seq_tiles=seq_tiles, ck=FFN_CHUNK),
        grid=(t // tm,),
        in_specs=in_specs,
        out_specs=(tok(d), conv_spec),
        out_shape=(jax.ShapeDtypeStruct((t, d), F32), conv_shape),
        scratch_shapes=scratch,
        compiler_params=_cparams(1),
        name="post_mixer",
    )(*args)


def _sample_attn_rows(q_ref, kn_ref, vn_ref, kt_ref, vt_ref, o_ref):
    nrows, nh, _, wbuf = kt_ref.shape
    r = lax.broadcasted_iota(jnp.int32, (HEAD_DIM, HEAD_DIM), 0)
    c = lax.broadcasted_iota(jnp.int32, (HEAD_DIM, HEAD_DIM), 1)
    diag = r == c
    dist = wbuf - lax.broadcasted_iota(jnp.int32, (1, wbuf), 1)
    for b in range(nrows):
        q = q_ref[b]
        kn = kn_ref[b]
        vn = vn_ref[b]
        s_new = jnp.sum(q * kn, axis=-1, keepdims=True)
        rows = []
        for h in range(nh):
            q_col = jnp.sum(jnp.where(diag, q[h:h + 1, :], 0.0), axis=1, keepdims=True)
            rows.append(jnp.sum(kt_ref[b, h] * q_col, axis=0, keepdims=True))
        s = jnp.concatenate(rows, axis=0)
        stats = []
        for dil in DILATIONS:
            mask = ((dist & (dil - 1)) == 0) & (dist <= BAND * dil)
            sg = jnp.where(mask, s, NEG_INF)
            m = jnp.maximum(jnp.max(sg, axis=1, keepdims=True), s_new)
            e = jnp.exp(sg - m)
            e_new = jnp.exp(s_new - m)
            stats.append((m, e, e_new, jnp.sum(e, axis=1, keepdims=True) + e_new))
        mm = jnp.maximum(jnp.maximum(stats[0][0], stats[1][0]), stats[2][0])
        cs = [jnp.exp(m - mm) for m, _, _, _ in stats]
        wgt = cs[0] * stats[0][1] + cs[1] * stats[1][1] + cs[2] * stats[2][1]
        w_new = cs[0] * stats[0][2] + cs[1] * stats[1][2] + cs[2] * stats[2][2]
        den = cs[0] * stats[0][3] + cs[1] * stats[1][3] + cs[2] * stats[2][3]
        outs = []
        for h in range(nh):
            o_col = jnp.sum(vt_ref[b, h] * wgt[h:h + 1, :], axis=1, keepdims=True)
            outs.append(jnp.sum(jnp.where(diag, o_col, 0.0), axis=0, keepdims=True))
        o_ref[b] = (jnp.concatenate(outs, axis=0) + w_new * vn) / den


def _smlstm_body(q_ref, k_ref, v_ref, ig_ref, lf_ref, c_ref, n_ref, m_ref, *refs):
    h_ref, co_ref, no_ref, mo_ref = refs[-4:]
    r = lax.broadcasted_iota(jnp.int32, (HEAD_DIM, HEAD_DIM), 0)
    l = lax.broadcasted_iota(jnp.int32, (HEAD_DIM, HEAD_DIM), 1)
    nh = q_ref.shape[1]
    for b in range(q_ref.shape[0]):
        q, k, v = q_ref[b], k_ref[b], v_ref[b]
        ig, lf, m = ig_ref[b], lf_ref[b], m_ref[b]
        n = n_ref[b]
        inter = lf + m
        m_t = jnp.maximum(inter, ig)
        dw = jnp.exp(ig - m_t)
        iw = jnp.exp(inter - m_t)
        sc = jnp.sum(q * k, axis=-1, keepdims=True) * dw
        cq = []
        for h in range(nh):
            c_h = c_ref[b, h]
            q_h = jnp.broadcast_to(q[h:h + 1, :], (8, HEAD_DIM)).astype(BF16)
            cq.append(_dot_nt(q_h, c_h.astype(BF16))[0:1, :])
            v_col = jnp.sum(jnp.where(r == l, v[h:h + 1, :], 0.0), axis=1, keepdims=True)
            co_ref[b, h] = iw[h:h + 1, :] * c_h + (dw[h:h + 1, :] * v_col) * k[h:h + 1, :]
        num = iw * jnp.concatenate(cq, axis=0) + sc * v
        den = iw * jnp.sum(n * q, axis=-1, keepdims=True) + sc
        h_ref[b] = num / jnp.maximum(jnp.abs(den), jnp.exp(-m_t))
        no_ref[b] = iw * n + dw * k
        mo_ref[b] = m_t


def _sample_mlstm(q, k, v, ig, lf, c, n, m, layer, c_acc, *, rows):
    b, nh, dh = q.shape
    depth = c.shape[0]
    row = pl.BlockSpec((rows, nh, dh), lambda i: (i, 0, 0))
    col = pl.BlockSpec((rows, nh, 1), lambda i: (i, 0, 0))
    lrow = pl.BlockSpec((None, rows, nh, dh), lambda i: (layer, i, 0, 0))
    lcol = pl.BlockSpec((None, rows, nh, 1), lambda i: (layer, i, 0, 0))
    lcst = pl.BlockSpec((None, rows, nh, dh, dh), lambda i: (layer, i, 0, 0, 0))
    args = [q, k, v, ig, lf, c, n, m]
    in_specs = [row, row, row, col, col, lcst, lrow, lcol]
    kwargs = {"input_output_aliases": {len(args): 1}}
    args.append(c_acc)
    in_specs.append(pl.BlockSpec(memory_space=pl.ANY))
    return pl.pallas_call(
        _smlstm_body,
        grid=(b // rows,),
        in_specs=in_specs,
        out_specs=(row, lcst, row, col),
        out_shape=(jax.ShapeDtypeStruct((b, nh, dh), F32), jax.ShapeDtypeStruct((depth, b, nh, dh, dh), F32),
                   jax.ShapeDtypeStruct((b, nh, dh), F32), jax.ShapeDtypeStruct((b, nh, 1), F32)),
        compiler_params=_cparams(1),
        name="sample_mlstm",
        **kwargs,
    )(*args)


def _rope_tables(pos):
    half = ROT_DIM // 2
    inv = ROPE_THETA ** (-jnp.arange(half, dtype=F32) / half)
    ang = pos.astype(F32)[:, None] * inv[None, :]
    cos, sin = jnp.cos(ang), jnp.sin(ang)
    n = pos.shape[0]
    rest = HEAD_DIM - ROT_DIM
    c = jnp.concatenate([cos, cos, jnp.ones((n, rest), F32)], axis=1)
    sa = jnp.concatenate([-sin, jnp.zeros((n, half + rest), F32)], axis=1)
    sb = jnp.concatenate([jnp.zeros((n, half), F32), sin, jnp.zeros((n, rest), F32)], axis=1)
    rep = LANES // HEAD_DIM
    return tuple(jnp.concatenate([t] * rep, axis=1) for t in (c, sa, sb))


def _rope_angles_t(pos):
    half = ROT_DIM // 2
    inv = ROPE_THETA ** (-jnp.arange(half, dtype=F32) / half)
    ang = pos.astype(F32)[:, None] * inv[None, :]
    return jnp.cos(ang).T, jnp.sin(ang).T


def kernel(x_prompt, x_sample, cache_win_k, cache_win_v, state_C, state_n, state_m, state_conv, p_prompt, p_sample,
           norm_mix, w_in, q_norm, k_norm, b_gates, out_norm_a, out_norm_b, w_out, norm_ffn, w_up, conv_w, conv_b,
           w_down, norm_ple, w_ple_gate, w_ple_proj):
    batch, seq, d = x_prompt.shape
    dec_batch, dec_seq, _ = x_sample.shape
    depth = w_in.shape[0]
    nh_a = cache_win_k.shape[3]
    nh_b = state_C.shape[2]
    wa, wb = nh_a * HEAD_DIM, nh_b * HEAD_DIM
    wbuf = cache_win_k.shape[2]
    win_max = BAND * DILATIONS[-1]
    keep = min(win_max, seq)
    assert dec_seq == 1 and wbuf == win_max and seq % win_max == 0
    assert w_in.shape[2] == 3 * wa + 4 * wb + 2 * nh_b and wa % LANES == 0 and wb % LANES == 0

    tm = min(TOKEN_TILE, seq)
    seq_tiles = seq // tm
    tp, ts = batch * seq, dec_batch * dec_seq

    cut = 3 * wa + 4 * wb
    w_in_t = jnp.transpose(w_in, (0, 2, 1))
    w_main = w_in_t[:, :cut, :].astype(BF16)
    w_gate = w_in_t[:, cut + 2 * nh_b - LANES:, :].astype(BF16)
    bg = jnp.concatenate([jnp.zeros((depth, GATE_LANE0), F32), b_gates], axis=1)[:, None, :]
    qn = jnp.tile(q_norm, (1, nh_a))[:, None, :]
    kn = jnp.tile(k_norm, (1, nh_a))[:, None, :]
    rows = lambda a: a[:, None, :]
    cols = lambda a: jnp.broadcast_to(a[:, 0, :, None], (depth, wa, LANES))
    inproj_params_s = (rows(norm_mix), w_main, w_gate, qn, kn, bg)
    inproj_params_p = (rows(norm_mix), w_main, w_gate, cols(qn), cols(kn), bg)
    post_params = (rows(out_norm_a), rows(out_norm_b), w_out.astype(BF16), rows(norm_ffn), w_up.astype(BF16),
                   conv_w, rows(conv_b), w_down.astype(BF16), rows(norm_ple), w_ple_gate.astype(BF16),
                   w_ple_proj.astype(BF16))
    pp = p_prompt.reshape(depth, tp, -1)
    ps = p_sample.reshape(depth, ts, -1)
    head_of_lane = jnp.arange(wa) // HEAD_DIM
    ind = (head_of_lane[:, None] == jnp.arange(LANES)[None, :]).astype(BF16)
    indt = ind.T
    tab_p = _rope_angles_t(jnp.arange(seq, dtype=jnp.int32))
    tab_s = tuple(jnp.broadcast_to(t, (ts, LANES))
                  for t in _rope_tables(PAST_LEN + jnp.arange(dec_seq, dtype=jnp.int32)))

    ckt = jnp.transpose(cache_win_k, (0, 1, 3, 4, 2))
    cvt = jnp.transpose(cache_win_v, (0, 1, 3, 4, 2))

    xp = x_prompt.reshape(tp, d)
    xs = x_sample.reshape(ts, d)
    outs = [[] for _ in range(12)]
    wk_acc = jnp.zeros((depth, batch, wa, keep), F32)
    wv_acc = jnp.zeros_like(wk_acc)
    cs_acc = jnp.zeros_like(state_C)
    state_m4 = state_m[..., None]
    for i in range(depth):
        q_rm, wk_acc, wv_acc, qbp, kbp, vbp, ogp, gt, k_rm, v_rm = _inproj(
            xp, i, *inproj_params_p, tab_p, (),
            tm=tm, seq_tiles=seq_tiles, batch=batch, mdtype=BF16, transpose_gates=True,
            window=(keep, depth, wk_acc, wv_acc))
        qa, ka, va, qb, kb, vb, og, ga = _inproj(
            xs, i, *inproj_params_s, tab_s, (ind, indt),
            tm=ts, seq_tiles=1, batch=dec_batch, mdtype=F32, transpose_gates=False)
        r3 = lambda a: a.reshape(batch, seq, a.shape[-1])
        hd = lambda a: a.reshape(dec_batch, -1, HEAD_DIM)
        oap = _attention(q_rm, k_rm, v_rm)
        hbp, c_new, n_new, m_new, oa = _mlstm(r3(qbp), r3(kbp), r3(vbp), gt, hd(qa), hd(ka), hd(va), ckt, cvt, i,
                                              ts=min(seq, MLSTM_TILE))
        xp, conv_new = _post(xp, oap.reshape(tp, wa), hbp.reshape(tp, wb), ogp, pp, i, post_params,
                             tm=tm, seq_tiles=seq_tiles, batch=batch)
        outs[2].append(c_new)
        outs[3].append(n_new)
        outs[4].append(m_new[..., 0])
        outs[5].append(conv_new)
        ig = ga[:, GATE_LANE0:GATE_LANE0 + nh_b, None]
        lf = ga[:, GATE_LANE0 + nh_b:, None]
        hb, cs_acc, n_new, m_new = _sample_mlstm(hd(qb), hd(kb), hd(vb), ig, lf, state_C, state_n, state_m4, i,
                                                 cs_acc, rows=SAMPLE_ROWS)
        xs, gate_new = _post(xs, oa.reshape(ts, wa), hb.reshape(ts, wb), og, ps, i, post_params,
                             prev=(state_conv[i][:, 0], state_conv[i][:, 1]), tm=ts)
        outs[6].append(ka.reshape(dec_batch, dec_seq, nh_a, HEAD_DIM))
        outs[7].append(va.reshape(dec_batch, dec_seq, nh_a, HEAD_DIM))
        outs[9].append(n_new)
        outs[10].append(m_new[..., 0])
        outs[11].append(jnp.stack([state_conv[i][:, 1], gate_new], axis=1))
    def window_out(acc):
        return jnp.transpose(acc.reshape(depth, batch, nh_a, HEAD_DIM, keep), (0, 1, 4, 2, 3))

    st = jnp.stack
    return ((xp.reshape(batch, seq, d), xs.reshape(dec_batch, dec_seq, d), window_out(wk_acc), window_out(wv_acc))
            + tuple(st(o) for o in outs[2:8]) + (cs_acc,) + tuple(st(o) for o in outs[9:]))
```
